```python
import math
import jax, jax.numpy as jnp
from jax import lax
import numpy as np

D_MODEL = 1024
BATCH = 8
SEQ = 4096
DEPTH = 2

D_MIX = D_MODEL
D_A = D_MIX // 2
D_B = D_MIX - D_A
HEAD_DIM = 64
H_A = D_A // HEAD_DIM
H_B = D_B // HEAD_DIM
N_KV = 2
GQA = H_B // N_KV
D_KV = N_KV * HEAD_DIM
CHUNK = 128
CMP_BLOCK = 32
CMP_STRIDE = 16
CMP_HIDDEN = 256
SEL_BLOCK = 64
N_SELECT = 16
WINDOW = 512
Q_BLOCK = 128
N_GATES = 3
D_IN = 2 * D_A + D_B + 6 * D_KV + N_GATES * H_B
D_FF = int(math.ceil(8 * D_MODEL / 3 / 256)) * 256
EPS = 1e-6
NEG = -1e30
FORCE = 1e4

kernel_name = "hybrid_gmlp_nsa_parallel_heads"


def rmsnorm(x, w):
    xf = x.astype(jnp.float32)
    y = xf * lax.rsqrt(jnp.mean(xf * xf, axis=-1, keepdims=True) + EPS)
    return (y * w.astype(jnp.float32)).astype(x.dtype)


def masked_softmax(s, mask):
    p = jax.nn.softmax(jnp.where(mask, s.astype(jnp.float32), NEG), axis=-1)
    return jnp.where(mask, p, 0.0)


def split_offsets():
    sizes = [D_A, D_A, D_B, D_KV, D_KV, D_KV, D_KV, D_KV, D_KV, N_GATES * H_B]
    return [int(o) for o in np.cumsum(sizes)[:-1]]


def gmlp_group(u, v, norm_w, ws, bs):
    B, S, _ = u.shape
    nc = S // CHUNK
    u = jax.nn.gelu(u).reshape(B, nc, CHUNK, H_A, HEAD_DIM)
    v = rmsnorm(jax.nn.gelu(v), norm_w).reshape(B, nc, CHUNK, H_A, HEAD_DIM)
    causal = jnp.tril(jnp.ones((CHUNK, CHUNK), dtype=bool))
    w = jnp.where(causal[None], ws, 0).astype(v.dtype)
    sv = jnp.einsum('hts,bcshd->bcthd', w, v) + bs.T.astype(v.dtype)[None, None, :, :, None]
    return (u * sv).reshape(B, S, D_A)


def compress(kv, pos, w1, w2):
    B, S = kv.shape[0], kv.shape[1]
    n_cmp = (S - CMP_BLOCK) // CMP_STRIDE + 1
    idx = jnp.arange(n_cmp)[:, None] * CMP_STRIDE + jnp.arange(CMP_BLOCK)[None, :]
    blk = kv[:, idx] + pos[None, None, :, None, :]
    blk = blk.transpose(0, 1, 3, 2, 4).reshape(B, n_cmp, N_KV, CMP_BLOCK * HEAD_DIM)
    return jax.nn.gelu(blk @ w1) @ w2


def nsa_group(q, k_c, v_c, k_s, v_s, k_w, v_w, g_raw,
              cmp_pos_k, cmp_pos_v, cmp_k_w1, cmp_k_w2, cmp_v_w1, cmp_v_w2, gate_b):
    B, S = q.shape[0], q.shape[1]
    q = (q * (HEAD_DIM ** -0.5)).reshape(B, S, N_KV, GQA, HEAD_DIM)
    kv_shape = (B, S, N_KV, HEAD_DIM)
    k_c, v_c, k_s, v_s, k_w, v_w = [a.reshape(kv_shape) for a in (k_c, v_c, k_s, v_s, k_w, v_w)]
    t = jnp.arange(S)
    n_qb = S // Q_BLOCK

    kc = compress(k_c, cmp_pos_k, cmp_k_w1, cmp_k_w2)
    vc = compress(v_c, cmp_pos_v, cmp_v_w1, cmp_v_w2)
    n_cmp = kc.shape[1]
    cmp_start = jnp.arange(n_cmp) * CMP_STRIDE
    cmp_mask = (cmp_start + CMP_BLOCK - 1)[None, :] <= t[:, None]
    s_cmp = jnp.einsum('bskgd,bnkd->bkgsn', q, kc)
    p_cmp = masked_softmax(s_cmp, cmp_mask)
    o_cmp = jnp.einsum('bkgsn,bnkd->bskgd', p_cmp.astype(vc.dtype), vc)

    n_blk = S // SEL_BLOCK
    n_sel = min(N_SELECT, n_blk)
    blk_start = jnp.arange(n_blk) * SEL_BLOCK
    overlap = ((cmp_start[:, None] < blk_start[None, :] + SEL_BLOCK)
               & (cmp_start[:, None] + CMP_BLOCK > blk_start[None, :])).astype(jnp.float32)
    imp = jnp.einsum('bkgsn,nj->bksj', p_cmp, overlap)
    cur = t // SEL_BLOCK
    j = jnp.arange(n_blk)
    valid = j[None, :] <= cur[:, None]
    forced = (j[None, :] == 0) | (j[None, :] == cur[:, None]) | (j[None, :] == cur[:, None] - 1)
    score = jnp.where(forced, FORCE, jnp.where(valid, imp, -FORCE))
    _, sel = lax.top_k(score, n_sel)
    sel = sel.transpose(0, 2, 1, 3)

    kb = k_s.reshape(B, n_blk, SEL_BLOCK, N_KV, HEAD_DIM).transpose(0, 3, 1, 2, 4)
    vb = v_s.reshape(B, n_blk, SEL_BLOCK, N_KV, HEAD_DIM).transpose(0, 3, 1, 2, 4)
    b_ix = jnp.arange(B)[:, None, None, None]
    h_ix = jnp.arange(N_KV)[None, None, :, None]

    def sel_block(args):
        qc, ic, tc = args
        kg = kb[b_ix, h_ix, ic]
        vg = vb[b_ix, h_ix, ic]
        pos = ic[..., None] * SEL_BLOCK + jnp.arange(SEL_BLOCK)
        mask = (pos <= tc[None, :, None, None, None]).reshape(B, Q_BLOCK, N_KV, 1, n_sel * SEL_BLOCK)
        s = jnp.einsum('bqkgd,bqknrd->bqkgnr', qc, kg).reshape(B, Q_BLOCK, N_KV, GQA, n_sel * SEL_BLOCK)
        p = masked_softmax(s, mask)
        vg = vg.reshape(B, Q_BLOCK, N_KV, n_sel * SEL_BLOCK, HEAD_DIM)
        return jnp.einsum('bqkgm,bqkmd->bqkgd', p.astype(vg.dtype), vg)

    qs = q.reshape(B, n_qb, Q_BLOCK, N_KV, GQA, HEAD_DIM).transpose(1, 0, 2, 3, 4, 5)
    sels = sel.reshape(B, n_qb, Q_BLOCK, N_KV, n_sel).transpose(1, 0, 2, 3, 4)
    ts = t.reshape(n_qb, Q_BLOCK)
    o_slc = lax.map(sel_block, (qs, sels, ts))
    o_slc = o_slc.transpose(1, 0, 2, 3, 4, 5).reshape(B, S, N_KV, GQA, HEAD_DIM)

    span = WINDOW + Q_BLOCK
    pad = ((0, 0), (WINDOW, 0), (0, 0), (0, 0))
    band = jnp.arange(n_qb)[:, None] * Q_BLOCK + jnp.arange(span)[None, :]
    kw = jnp.pad(k_w, pad)[:, band]
    vw = jnp.pad(v_w, pad)[:, band]
    kpos = band - WINDOW
    diff = ts[:, :, None] - kpos[:, None, :]
    win_mask = (diff >= 0) & (diff < WINDOW) & (kpos[:, None, :] >= 0)
    qw = q.reshape(B, n_qb, Q_BLOCK, N_KV, GQA, HEAD_DIM)
    s_win = jnp.einsum('bcqkgd,bcmkd->bckgqm', qw, kw)
    p_win = masked_softmax(s_win, win_mask[None, :, None, None])
    o_win = jnp.einsum('bckgqm,bcmkd->bcqkgd', p_win.astype(vw.dtype), vw).reshape(B, S, N_KV, GQA, HEAD_DIM)

    g = jax.nn.sigmoid(g_raw + gate_b).reshape(B, S, N_GATES, N_KV, GQA, 1)
    o = g[:, :, 0] * o_cmp + g[:, :, 1] * o_slc + g[:, :, 2] * o_win
    return o.reshape(B, S, D_B)


def setup_inputs(seed: int = 0) -> dict:
    key = jax.random.key(seed)
    ks = iter(jax.random.split(key, 32))

    def nrm(shape, scale):
        return jax.random.normal(next(ks), shape, jnp.float32) * scale

    def gain(shape):
        return 1.0 + nrm(shape, 0.01)

    L = DEPTH
    return {
        "x": nrm((BATCH, SEQ, D_MODEL), 1.0),
        "norm_mix_w": gain((L, D_MODEL)),
        "w_in": nrm((L, D_MODEL, D_IN), D_MODEL ** -0.5),
        "gmlp_norm_w": gain((L, D_A)),
        "gmlp_ws": nrm((L, H_A, CHUNK, CHUNK), 0.5 * CHUNK ** -0.5),
        "gmlp_bs": 1.0 + nrm((L, H_A, CHUNK), 0.01),
        "cmp_pos_k": nrm((L, CMP_BLOCK, HEAD_DIM), 0.1),
        "cmp_pos_v": nrm((L, CMP_BLOCK, HEAD_DIM), 0.1),
        "cmp_k_w1": nrm((L, CMP_BLOCK * HEAD_DIM, CMP_HIDDEN), (CMP_BLOCK * HEAD_DIM) ** -0.5),
        "cmp_k_w2": nrm((L, CMP_HIDDEN, HEAD_DIM), CMP_HIDDEN ** -0.5),
        "cmp_v_w1": nrm((L, CMP_BLOCK * HEAD_DIM, CMP_HIDDEN), (CMP_BLOCK * HEAD_DIM) ** -0.5),
        "cmp_v_w2": nrm((L, CMP_HIDDEN, HEAD_DIM), CMP_HIDDEN ** -0.5),
        "gate_b": nrm((L, N_GATES * H_B), 0.01),
        "out_norm_a_w": gain((L, D_A)),
        "out_norm_b_w": gain((L, D_B)),
        "w_o": nrm((L, D_MIX, D_MODEL), D_MIX ** -0.5),
        "norm_ffn_w": gain((L, D_MODEL)),
        "w_gate": nrm((L, D_MODEL, D_FF), D_MODEL ** -0.5),
        "w_up": nrm((L, D_MODEL, D_FF), D_MODEL ** -0.5),
        "w_down": nrm((L, D_FF, D_MODEL), D_FF ** -0.5),
        "final_norm_w": gain((D_MODEL,)),
    }


def reference(x, norm_mix_w, w_in, gmlp_norm_w, gmlp_ws, gmlp_bs,
              cmp_pos_k, cmp_pos_v, cmp_k_w1, cmp_k_w2, cmp_v_w1, cmp_v_w2,
              gate_b, out_norm_a_w, out_norm_b_w, w_o, norm_ffn_w,
              w_gate, w_up, w_down, final_norm_w):
    offsets = split_offsets()
    for l in range(DEPTH):
        h = rmsnorm(x, norm_mix_w[l])
        z = h @ w_in[l]
        u, v, q, k_c, v_c, k_s, v_s, k_w, v_w, g_raw = jnp.split(z, offsets, axis=-1)
        a = gmlp_group(u, v, gmlp_norm_w[l], gmlp_ws[l], gmlp_bs[l])
        b = nsa_group(q, k_c, v_c, k_s, v_s, k_w, v_w, g_raw,
                      cmp_pos_k[l], cmp_pos_v[l], cmp_k_w1[l], cmp_k_w2[l],
                      cmp_v_w1[l], cmp_v_w2[l], gate_b[l])
        mix = jnp.concatenate([rmsnorm(a, out_norm_a_w[l]), rmsnorm(b, out_norm_b_w[l])], axis=-1)
        x = x + mix @ w_o[l]
        h = rmsnorm(x, norm_ffn_w[l])
        x = x + (jax.nn.silu(h @ w_gate[l]) * (h @ w_up[l])) @ w_down[l]
    return rmsnorm(x, final_norm_w)
```

```python
import functools

import jax
import jax.numpy as jnp
import numpy as np
from jax import lax
from jax.experimental import pallas as pl
from jax.experimental.pallas import tpu as pltpu

D_MODEL = 1024
DEPTH = 2
D_A = 512
D_B = 512
HEAD_DIM = 64
H_A = D_A // HEAD_DIM
H_B = D_B // HEAD_DIM
N_KV = 2
GQA = H_B // N_KV
D_KV = N_KV * HEAD_DIM
CHUNK = 128
CMP_BLOCK = 32
CMP_STRIDE = 16
CMP_HIDDEN = 256
SEL_BLOCK = 64
N_SELECT = 16
WINDOW = 512
Q_BLOCK = 128
N_GATES = 3
D_FF = 2816
EPS = 1e-6
NEG = -1e30
FORCE = 1e4

LANES = 128
SUBLANES = 8
VMEM_LIMIT_BYTES = 56 * 1024 * 1024

TM_PROJ = 512
TM_FFN = 512
FF_CHUNK = 256
SEL_TILE = 256
WIN_SPAN = WINDOW + Q_BLOCK

F32 = jnp.float32
BF16 = jnp.bfloat16


def _gelu_tanh(x):
    c = np.float32(np.sqrt(2.0 / np.pi))
    return x * (0.5 * (1.0 + jnp.tanh(c * (x + 0.044715 * (x * x * x)))))


def _rms_scale(x):
    return lax.rsqrt(jnp.mean(x * x, axis=-1, keepdims=True) + EPS)


def _dot(a, b):
    return jnp.dot(a, b, preferred_element_type=F32)


def _dot_nt(a, b):
    return lax.dot_general(a, b, (((1,), (1,)), ((), ())), preferred_element_type=F32)


def _in_proj_kernel(x_ref, nw_ref, wuv_ref, wq_ref, wkv_ref, wg_ref,
                    uv_ref, q_ref, kv_ref, g_ref):
    x = x_ref[...]
    h = (x * _rms_scale(x) * nw_ref[...]).astype(BF16)
    uv_ref[...] = _dot(h, wuv_ref[...])
    q_ref[...] = (_dot(h, wq_ref[...]) * (HEAD_DIM ** -0.5)).astype(BF16)
    kv_ref[...] = _dot(h, wkv_ref[...]).astype(BF16)
    g_ref[...] = _dot(h, wg_ref[...])


def _in_proj(x2d, nw, wuv, wq, wkv, wg):
    t = x2d.shape[0]
    const = lambda i: (0, 0)
    row = lambda i: (i, 0)
    return pl.pallas_call(
        _in_proj_kernel,
        grid=(t // TM_PROJ,),
        in_specs=[
            pl.BlockSpec((TM_PROJ, D_MODEL), row),
            pl.BlockSpec((1, D_MODEL), const),
            pl.BlockSpec((D_MODEL, 2 * D_A), const),
            pl.BlockSpec((D_MODEL, D_B), const),
            pl.BlockSpec((D_MODEL, 6 * D_KV), const),
            pl.BlockSpec((D_MODEL, LANES), const),
        ],
        out_specs=[
            pl.BlockSpec((TM_PROJ, 2 * D_A), row),
            pl.BlockSpec((TM_PROJ, D_B), row),
            pl.BlockSpec((TM_PROJ, 6 * D_KV), row),
            pl.BlockSpec((TM_PROJ, LANES), row),
        ],
        out_shape=[
            jax.ShapeDtypeStruct((t, 2 * D_A), F32),
            jax.ShapeDtypeStruct((t, D_B), BF16),
            jax.ShapeDtypeStruct((t, 6 * D_KV), BF16),
            jax.ShapeDtypeStruct((t, LANES), F32),
        ],
        compiler_params=pltpu.CompilerParams(
            dimension_semantics=("parallel",), vmem_limit_bytes=VMEM_LIMIT_BYTES),
        name="in_proj",
    )(x2d, nw, wuv, wq, wkv, wg)


def _gmlp_kernel(u_ref, v_ref, nw_ref, ws_ref, bst_ref, onw_ref, o_ref):
    u = _gelu_tanh(u_ref[0])
    v = _gelu_tanh(v_ref[0])
    vn = (v * _rms_scale(v) * nw_ref[...]).astype(BF16)
    row = lax.broadcasted_iota(jnp.int32, (CHUNK, CHUNK), 0)
    col = lax.broadcasted_iota(jnp.int32, (CHUNK, CHUNK), 1)
    causal = col <= row
    low_half = col < HEAD_DIM
    bst = bst_ref[...]
    tiles = []
    for i in range(D_A // LANES):
        vt = vn[:, i * LANES:(i + 1) * LANES]
        pair = []
        for hh in (2 * i, 2 * i + 1):
            w = jnp.where(causal, ws_ref[hh], 0.0).astype(BF16)
            pair.append(_dot(w, vt) + bst[:, hh:hh + 1])
        tiles.append(jnp.where(low_half, pair[0], pair[1]))
    a = u * jnp.concatenate(tiles, axis=1)
    o_ref[0] = (a * _rms_scale(a) * onw_ref[...]).astype(BF16)


def _gmlp(uv3, nw, ws, bst, onw):
    b, s, _ = uv3.shape
    const2 = lambda bi, ci: (0, 0)
    return pl.pallas_call(
        _gmlp_kernel,
        grid=(b, s // CHUNK),
        in_specs=[
            pl.BlockSpec((1, CHUNK, D_A), lambda bi, ci: (bi, ci, 0)),
            pl.BlockSpec((1, CHUNK, D_A), lambda bi, ci: (bi, ci, 1)),
            pl.BlockSpec((1, D_A), const2),
            pl.BlockSpec((H_A, CHUNK, CHUNK), lambda bi, ci: (0, 0, 0)),
            pl.BlockSpec((CHUNK, LANES), const2),
            pl.BlockSpec((1, D_A), const2),
        ],
        out_specs=pl.BlockSpec((1, CHUNK, D_A), lambda bi, ci: (bi, ci, 0)),
        out_shape=jax.ShapeDtypeStruct((b, s, D_A), BF16),
        compiler_params=pltpu.CompilerParams(
            dimension_semantics=("parallel", "parallel"), vmem_limit_bytes=VMEM_LIMIT_BYTES),
        name="gmlp",
    )(uv3, uv3, nw, ws, bst, onw)


def _compress_kernel(g_ref, pos_ref, w1_ref, w2_ref, o_ref):
    half = CMP_STRIDE * HEAD_DIM
    g = g_ref[0, 0, 0].astype(F32)
    ga = (g + pos_ref[0, 0:1, :]).astype(BF16)
    gb = (g + pos_ref[0, 1:2, :]).astype(BF16)
    h1 = _dot(ga, w1_ref[0, :half, :])
    h2 = _dot(gb, w1_ref[0, half:, :])
    n_grp = h2.shape[0]
    h2_next = pltpu.roll(h2, n_grp - 1, 0)
    hid = _gelu_tanh(h1 + h2_next).astype(BF16)
    o_ref[0, 0, 0] = _dot(hid, w2_ref[0]).astype(BF16)


def _compress(groups, pos, w1, w2):
    b, _, _, n_grp, width = groups.shape
    return pl.pallas_call(
        _compress_kernel,
        grid=(b, 2, N_KV),
        in_specs=[
            pl.BlockSpec((1, 1, 1, n_grp, width), lambda bi, ti, hi: (bi, ti, hi, 0, 0)),
            pl.BlockSpec((1, 2, width), lambda bi, ti, hi: (ti, 0, 0)),
            pl.BlockSpec((1, 2 * width, CMP_HIDDEN), lambda bi, ti, hi: (ti, 0, 0)),
            pl.BlockSpec((1, CMP_HIDDEN, HEAD_DIM), lambda bi, ti, hi: (ti, 0, 0)),
        ],
        out_specs=pl.BlockSpec((1, 1, 1, n_grp, HEAD_DIM), lambda bi, ti, hi: (bi, ti, hi, 0, 0)),
        out_shape=jax.ShapeDtypeStruct((b, 2, N_KV, n_grp, HEAD_DIM), BF16),
        compiler_params=pltpu.CompilerParams(
            dimension_semantics=("parallel", "parallel", "parallel"),
            vmem_limit_bytes=VMEM_LIMIT_BYTES),
        name="compress",
    )(groups, pos, w1, w2)


def _masked_softmax_rows(s, mask):
    s = jnp.where(mask, s, NEG)
    m = jnp.max(s, axis=-1, keepdims=True)
    p = jnp.where(mask, jnp.exp(s - m), 0.0)
    l = jnp.sum(p, axis=-1, keepdims=True)
    return p / jnp.where(l == 0.0, 1.0, l)


def _per_head(x, fn):
    return jnp.concatenate(
        [fn(hh // GQA, x[hh * Q_BLOCK:(hh + 1) * Q_BLOCK]) for hh in range(H_B)], axis=0)


def _nsa_kernel(q_ref, kc_ref, vc_ref, ks_ref, vs_ref, kw_ref, vw_ref, g_ref, gb_ref,
                onw_ref, place_ref, ovl_ref, o_ref):
    c = pl.program_id(1)
    t0 = c * Q_BLOCK
    n_cmp_pad = kc_ref.shape[1]

    qp = _dot(q_ref[0], place_ref[...]).astype(BF16)
    q_pad = jnp.concatenate(
        [qp[:, hh * LANES:(hh + 1) * LANES] for hh in range(H_B)], axis=0)

    tq = t0 + lax.broadcasted_iota(jnp.int32, (Q_BLOCK, 1), 0)

    n_idx = lax.broadcasted_iota(jnp.int32, (1, n_cmp_pad), 1)
    cmp_mask = (n_idx * CMP_STRIDE + (CMP_BLOCK - 1)) <= tq
    s_c = _dot_nt(q_pad, kc_ref[0])
    p_c = jnp.concatenate(
        [_masked_softmax_rows(s_c[hh * Q_BLOCK:(hh + 1) * Q_BLOCK], cmp_mask)
         for hh in range(H_B)], axis=0)
    o_c = _dot(p_c.astype(BF16), vc_ref[0])

    n_blk_pad = ovl_ref.shape[1]
    j_idx = lax.broadcasted_iota(jnp.int32, (SUBLANES, Q_BLOCK), 0)
    cur = (t0 + lax.broadcasted_iota(jnp.int32, (SUBLANES, Q_BLOCK), 1)) // SEL_BLOCK
    n_grp = (S_BLOCKS + SUBLANES - 1) // SUBLANES
    sel_q = []
    for kh in range(N_KV):
        p_sum = p_c[(kh * GQA) * Q_BLOCK:(kh * GQA + 1) * Q_BLOCK]
        for g in range(1, GQA):
            p_sum = p_sum + p_c[(kh * GQA + g) * Q_BLOCK:(kh * GQA + g + 1) * Q_BLOCK]
        imp = jnp.dot(p_sum, ovl_ref[...], preferred_element_type=F32,
                      precision=lax.Precision.HIGHEST)
        imp_t = imp.T
        score = []
        for r in range(n_grp):
            j = j_idx + r * SUBLANES
            forced = (j == 0) | (j == cur) | (j == cur - 1)
            sc = jnp.where(j <= cur, imp_t[r * SUBLANES:(r + 1) * SUBLANES], -FORCE)
            score.append(jnp.where(forced, FORCE, sc))
        sel_rows = []
        for r in range(n_grp):
            cnt = jnp.zeros((SUBLANES, Q_BLOCK), jnp.int32)
            for jp in range(S_BLOCKS):
                other = score[jp // SUBLANES][jp % SUBLANES:jp % SUBLANES + 1, :]
                rp = jp // SUBLANES
                ge = (other >= score[r]).astype(jnp.int32)
                gt = (other > score[r]).astype(jnp.int32)
                if rp < r:
                    ahead = ge
                elif rp > r:
                    ahead = gt
                else:
                    ahead = jnp.where(j_idx > (jp % SUBLANES), ge, gt)
                cnt = cnt + ahead
            sel_rows.append(jnp.where(cnt < N_SELECT, 1.0, 0.0))
        sel_t = jnp.concatenate(
            sel_rows + [jnp.zeros((n_blk_pad - S_BLOCKS, Q_BLOCK), F32)], axis=0)
        sel_q.append(sel_t.T.astype(BF16))

    blk_row = lax.broadcasted_iota(jnp.int32, (n_blk_pad, SEL_TILE), 0)
    key_blk = lax.broadcasted_iota(jnp.int32, (n_blk_pad, SEL_TILE), 1) // SEL_BLOCK
    key_off = lax.broadcasted_iota(jnp.int32, (1, SEL_TILE), 1)

    def sel_body(i, carry):
        m, l, acc = carry
        k0 = pl.multiple_of(i * SEL_TILE, SEL_TILE)
        kt = ks_ref[0, pl.ds(k0, SEL_TILE), :]
        vt = vs_ref[0, pl.ds(k0, SEL_TILE), :]
        s = _dot_nt(q_pad, kt)
        expand = jnp.where(blk_row == key_blk + i * (SEL_TILE // SEL_BLOCK), 1.0, 0.0).astype(BF16)
        causal = (k0 + key_off) <= tq
        masks = [(_dot(sel_q[kh], expand) > 0.5) & causal for kh in range(N_KV)]
        s = _per_head(s, lambda kh, rows: jnp.where(masks[kh], rows, NEG))
        m_new = jnp.maximum(m, jnp.max(s, axis=-1, keepdims=True))
        alpha = jnp.exp(m - m_new)
        p = jnp.exp(s - m_new)
        l_new = alpha * l + jnp.sum(p, axis=-1, keepdims=True)
        acc_new = alpha * acc + _dot(p.astype(BF16), vt)
        return m_new, l_new, acc_new

    rows = H_B * Q_BLOCK
    n_tiles = (t0 + Q_BLOCK - 1) // SEL_TILE + 1
    m0 = jnp.full((rows, 1), NEG, F32)
    l0 = jnp.zeros((rows, 1), F32)
    acc0 = jnp.zeros((rows, LANES), F32)
    _, l_s, acc_s = lax.fori_loop(0, n_tiles, sel_body, (m0, l0, acc0))
    o_s = acc_s / l_s

    w0 = pl.multiple_of(jnp.maximum(t0 - WINDOW, 0), Q_BLOCK)
    kw = kw_ref[0, pl.ds(w0, WIN_SPAN), :]
    vw = vw_ref[0, pl.ds(w0, WIN_SPAN), :]
    diff = tq - (w0 + lax.broadcasted_iota(jnp.int32, (1, WIN_SPAN), 1))
    win_mask = (diff >= 0) & (diff < WINDOW)
    s_w = _dot_nt(q_pad, kw)
    s_w = _per_head(s_w, lambda kh, rows_: jnp.where(win_mask, rows_, NEG))
    m_w = jnp.max(s_w, axis=-1, keepdims=True)
    p_w = jnp.exp(s_w - m_w)
    l_w = jnp.sum(p_w, axis=-1, keepdims=True)
    o_w = _dot(p_w.astype(BF16), vw) / l_w

    gates = jax.nn.sigmoid(g_ref[0] + gb_ref[...])
    low_half = lax.broadcasted_iota(jnp.int32, (Q_BLOCK, LANES), 1) < HEAD_DIM
    tiles = []
    for i in range(D_B // LANES):
        pair = []
        for hh in (2 * i, 2 * i + 1):
            sl = slice(hh * Q_BLOCK, (hh + 1) * Q_BLOCK)
            comb = (gates[:, hh:hh + 1] * o_c[sl]
                    + gates[:, H_B + hh:H_B + hh + 1] * o_s[sl]
                    + gates[:, 2 * H_B + hh:2 * H_B + hh + 1] * o_w[sl])
            if hh // GQA != hh % 2:
                comb = pltpu.roll(comb, HEAD_DIM, 1)
            pair.append(comb)
        tiles.append(jnp.where(low_half, pair[0], pair[1]))
    bmix = jnp.concatenate(tiles, axis=1)
    o_ref[0] = (bmix * _rms_scale(bmix) * onw_ref[...]).astype(BF16)


S_BLOCKS = 64


def _nsa(q3, kcv, kv3, g3, gate_b, onw, place, ovl):
    b, s, _ = q3.shape
    n_cmp_pad = kcv.shape[2]
    const2 = lambda bi, ci: (0, 0)
    kv_spec = lambda col: pl.BlockSpec((1, s, D_KV), lambda bi, ci: (bi, 0, col))
    return pl.pallas_call(
        _nsa_kernel,
        grid=(b, s // Q_BLOCK),
        in_specs=[
            pl.BlockSpec((1, Q_BLOCK, D_B), lambda bi, ci: (bi, ci, 0)),
            pl.BlockSpec((None, 1, n_cmp_pad, D_KV), lambda bi, ci: (bi, 0, 0, 0)),
            pl.BlockSpec((None, 1, n_cmp_pad, D_KV), lambda bi, ci: (bi, 1, 0, 0)),
            kv_spec(2), kv_spec(3), kv_spec(4), kv_spec(5),
            pl.BlockSpec((1, Q_BLOCK, LANES), lambda bi, ci: (bi, ci, 0)),
            pl.BlockSpec((1, LANES), const2),
            pl.BlockSpec((1, D_B), const2),
            pl.BlockSpec(place.shape, const2),
            pl.BlockSpec(ovl.shape, const2),
        ],
        out_specs=pl.BlockSpec((1, Q_BLOCK, D_B), lambda bi, ci: (bi, ci, 0)),
        out_shape=jax.ShapeDtypeStruct((b, s, D_B), BF16),
        compiler_params=pltpu.CompilerParams(
            dimension_semantics=("parallel", "parallel"), vmem_limit_bytes=VMEM_LIMIT_BYTES),
        name="nsa",
    )(q3, kcv, kcv, kv3, kv3, kv3, kv3, g3, gate_b, onw, place, ovl)


def _mix_ffn_kernel(x_ref, a_ref, b_ref, nw_ref, fnw_ref, woa_hbm, wob_hbm, wg_hbm, wu_hbm,
                    wd_hbm, o_ref, woa_ref, wob_ref, wg_ref, wu_ref, wd_ref, sem, *, final_norm):
    @pl.when(pl.program_id(0) == 0)
    def _load_weights():
        pairs = ((woa_hbm, woa_ref), (wob_hbm, wob_ref), (wg_hbm, wg_ref), (wu_hbm, wu_ref),
                 (wd_hbm, wd_ref))
        copies = [pltpu.make_async_copy(src, dst, sem.at[i]) for i, (src, dst) in enumerate(pairs)]
        for cp in copies:
            cp.start()
        for cp in copies:
            cp.wait()

    x1 = x_ref[...] + _dot(a_ref[...], woa_ref[...]) + _dot(b_ref[...], wob_ref[...])
    h = (x1 * _rms_scale(x1) * nw_ref[...]).astype(BF16)
    gate = _dot(h, wg_ref[...])
    up = _dot(h, wu_ref[...])
    act = (gate * jax.nn.sigmoid(gate) * up).astype(BF16)
    acc = x1 + _dot(act, wd_ref[...])
    if final_norm:
        acc = acc * _rms_scale(acc) * fnw_ref[...]
    o_ref[...] = acc


def _mix_ffn(x2d, a2d, b2d, woa, wob, nw, wg, wu, wd, fnw, final_norm):
    t = x2d.shape[0]
    row = lambda i: (i, 0)
    const = lambda i: (0, 0)
    weights = (woa, wob, wg, wu, wd)
    in_hbm = pl.BlockSpec(memory_space=pl.ANY)
    return pl.pallas_call(
        functools.partial(_mix_ffn_kernel, final_norm=final_norm),
        grid=(t // TM_FFN,),
        in_specs=[
            pl.BlockSpec((TM_FFN, D_MODEL), row),
            pl.BlockSpec((TM_FFN, D_A), row),
            pl.BlockSpec((TM_FFN, D_B), row),
            pl.BlockSpec((1, D_MODEL), const),
            pl.BlockSpec((1, D_MODEL), const),
        ] + [in_hbm] * len(weights),
        out_specs=pl.BlockSpec((TM_FFN, D_MODEL), row),
        out_shape=jax.ShapeDtypeStruct((t, D_MODEL), F32),
        scratch_shapes=[pltpu.VMEM(wt.shape, wt.dtype) for wt in weights]
        + [pltpu.SemaphoreType.DMA((len(weights),))],
        compiler_params=pltpu.CompilerParams(
            dimension_semantics=("arbitrary",), vmem_limit_bytes=VMEM_LIMIT_BYTES),
        name="mix_ffn",
    )(x2d, a2d, b2d, nw, fnw, *weights)


def _placement_matrix():
    p = np.zeros((D_B, H_B * LANES), np.float32)
    for hh in range(H_B):
        for d in range(HEAD_DIM):
            p[hh * HEAD_DIM + d, hh * LANES + (hh // GQA) * HEAD_DIM + d] = 1.0
    return jnp.asarray(p, BF16)


def _overlap_matrix(n_cmp_pad, n_cmp, n_blk, n_blk_pad):
    cmp_start = np.arange(n_cmp_pad) * CMP_STRIDE
    blk_start = np.arange(n_blk_pad) * SEL_BLOCK
    ov = ((cmp_start[:, None] < blk_start[None, :] + SEL_BLOCK)
          & (cmp_start[:, None] + CMP_BLOCK > blk_start[None, :]))
    ov &= (np.arange(n_cmp_pad)[:, None] < n_cmp) & (np.arange(n_blk_pad)[None, :] < n_blk)
    return jnp.asarray(ov.astype(np.float32))


def kernel(x, norm_mix_w, w_in, gmlp_norm_w, gmlp_ws, gmlp_bs, cmp_pos_k, cmp_pos_v, cmp_k_w1,
           cmp_k_w2, cmp_v_w1, cmp_v_w2, gate_b, out_norm_a_w, out_norm_b_w, w_o, norm_ffn_w,
           w_gate, w_up, w_down, final_norm_w):
    b, s, d = x.shape
    assert d == D_MODEL and s // SEL_BLOCK == S_BLOCKS and s % Q_BLOCK == 0
    assert (b * s) % TM_PROJ == 0 and (b * s) % TM_FFN == 0
    t = b * s
    n_grp = s // CMP_STRIDE
    n_cmp = (s - CMP_BLOCK) // CMP_STRIDE + 1
    half = CMP_STRIDE * HEAD_DIM
    place = _placement_matrix()
    ovl = _overlap_matrix(n_grp, n_cmp, S_BLOCKS, LANES)
    o_uv, o_q, o_kv = 2 * D_A, 2 * D_A + D_B, 2 * D_A + D_B + 6 * D_KV

    x2d = x.reshape(t, d)
    for l in range(DEPTH):
        w = w_in[l]
        wg_pad = jnp.pad(w[:, o_kv:], ((0, 0), (0, LANES - N_GATES * H_B)))
        uv, q, kv, g_raw = _in_proj(
            x2d, norm_mix_w[l][None, :], w[:, :o_uv].astype(BF16), w[:, o_uv:o_q].astype(BF16),
            w[:, o_q:o_kv].astype(BF16), wg_pad.astype(BF16))

        bst = jnp.pad(gmlp_bs[l].T, ((0, 0), (0, LANES - H_A)))
        a_n = _gmlp(uv.reshape(b, s, 2 * D_A), gmlp_norm_w[l][None, :], gmlp_ws[l], bst,
                    out_norm_a_w[l][None, :])

        kv3 = kv.reshape(b, s, 6 * D_KV)
        groups = kv3[:, :, :2 * D_KV].reshape(b, n_grp, CMP_STRIDE, 2, N_KV, HEAD_DIM)
        groups = groups.transpose(0, 3, 4, 1, 2, 5).reshape(b, 2, N_KV, n_grp, half)
        pos = jnp.stack([cmp_pos_k[l].reshape(2, half), cmp_pos_v[l].reshape(2, half)])
        w1 = jnp.stack([cmp_k_w1[l], cmp_v_w1[l]]).astype(BF16)
        w2 = jnp.stack([cmp_k_w2[l], cmp_v_w2[l]]).astype(BF16)
        kcv = _compress(groups, pos, w1, w2)
        kcv = kcv.transpose(0, 1, 3, 2, 4).reshape(b, 2, n_grp, D_KV)

        gb_pad = jnp.pad(gate_b[l], (0, LANES - N_GATES * H_B))[None, :]
        b_n = _nsa(q.reshape(b, s, D_B), kcv, kv3, g_raw.reshape(b, s, LANES), gb_pad,
                   out_norm_b_w[l][None, :], place, ovl)

        x2d = _mix_ffn(
            x2d, a_n.reshape(t, D_A), b_n.reshape(t, D_B),
            w_o[l][:D_A].astype(BF16), w_o[l][D_A:].astype(BF16), norm_ffn_w[l][None, :],
            w_gate[l].astype(BF16), w_up[l].astype(BF16), w_down[l].astype(BF16),
            final_norm_w[None, :], final_norm=(l == DEPTH - 1))
    return x2d.reshape(b, s, d)
```

```python
import functools

import jax
import jax.numpy as jnp
import numpy as np
from jax import lax
from jax.experimental import pallas as pl
from jax.experimental.pallas import tpu as pltpu

D_MODEL = 1024
DEPTH = 2
D_A = 512
D_B = 512
HEAD_DIM = 64
H_A = D_A // HEAD_DIM
H_B = D_B // HEAD_DIM
N_KV = 2
GQA = H_B // N_KV
D_KV = N_KV * HEAD_DIM
CHUNK = 128
CMP_BLOCK = 32
CMP_STRIDE = 16
CMP_HIDDEN = 256
SEL_BLOCK = 64
N_SELECT = 16
WINDOW = 512
Q_BLOCK = 128
N_GATES = 3
D_FF = 2816
EPS = 1e-6
NEG = -1e30
FORCE = 1e4

LANES = 128
SUBLANES = 8
VMEM_LIMIT_BYTES = 56 * 1024 * 1024

TM_PROJ = 512
TM_FFN = 512
FF_CHUNK = 256
SEL_TILE = 256
WIN_SPAN = WINDOW + Q_BLOCK

F32 = jnp.float32
BF16 = jnp.bfloat16


def _gelu_tanh(x):
    c = np.float32(np.sqrt(2.0 / np.pi))
    return x * (0.5 * (1.0 + jnp.tanh(c * (x + 0.044715 * (x * x * x)))))


def _rms_scale(x):
    return lax.rsqrt(jnp.mean(x * x, axis=-1, keepdims=True) + EPS)


def _dot(a, b):
    return jnp.dot(a, b, preferred_element_type=F32)


def _dot_nt(a, b):
    return lax.dot_general(a, b, (((1,), (1,)), ((), ())), preferred_element_type=F32)


def _in_proj_kernel(x_ref, nw_ref, wuv_ref, wq_ref, wkv_ref, wg_ref,
                    uv_ref, q_ref, kv_ref, g_ref):
    x = x_ref[...]
    h = (x * _rms_scale(x) * nw_ref[...]).astype(BF16)
    uv_ref[...] = _dot(h, wuv_ref[...])
    q_ref[...] = (_dot(h, wq_ref[...]) * (HEAD_DIM ** -0.5)).astype(BF16)
    kv_ref[...] = _dot(h, wkv_ref[...]).astype(BF16)
    g_ref[...] = _dot(h, wg_ref[...])


def _in_proj(x2d, nw, wuv, wq, wkv, wg):
    t = x2d.shape[0]
    const = lambda i: (0, 0)
    row = lambda i: (i, 0)
    return pl.pallas_call(
        _in_proj_kernel,
        grid=(t // TM_PROJ,),
        in_specs=[
            pl.BlockSpec((TM_PROJ, D_MODEL), row),
            pl.BlockSpec((1, D_MODEL), const),
            pl.BlockSpec((D_MODEL, 2 * D_A), const),
            pl.BlockSpec((D_MODEL, D_B), const),
            pl.BlockSpec((D_MODEL, 6 * D_KV), const),
            pl.BlockSpec((D_MODEL, LANES), const),
        ],
        out_specs=[
            pl.BlockSpec((TM_PROJ, 2 * D_A), row),
            pl.BlockSpec((TM_PROJ, D_B), row),
            pl.BlockSpec((TM_PROJ, 6 * D_KV), row),
            pl.BlockSpec((TM_PROJ, LANES), row),
        ],
        out_shape=[
            jax.ShapeDtypeStruct((t, 2 * D_A), F32),
            jax.ShapeDtypeStruct((t, D_B), BF16),
            jax.ShapeDtypeStruct((t, 6 * D_KV), BF16),
            jax.ShapeDtypeStruct((t, LANES), F32),
        ],
        compiler_params=pltpu.CompilerParams(
            dimension_semantics=("parallel",), vmem_limit_bytes=VMEM_LIMIT_BYTES),
        name="in_proj",
    )(x2d, nw, wuv, wq, wkv, wg)


def _gmlp_kernel(u_ref, v_ref, nw_ref, ws_ref, bst_ref, onw_ref, o_ref):
    u = _gelu_tanh(u_ref[0])
    v = _gelu_tanh(v_ref[0])
    vn = (v * _rms_scale(v) * nw_ref[...]).astype(BF16)
    row = lax.broadcasted_iota(jnp.int32, (CHUNK, CHUNK), 0)
    col = lax.broadcasted_iota(jnp.int32, (CHUNK, CHUNK), 1)
    causal = col <= row
    low_half = col < HEAD_DIM
    bst = bst_ref[...]
    tiles = []
    for i in range(D_A // LANES):
        vt = vn[:, i * LANES:(i + 1) * LANES]
        pair = []
        for hh in (2 * i, 2 * i + 1):
            w = jnp.where(causal, ws_ref[hh], 0.0).astype(BF16)
            pair.append(_dot(w, vt) + bst[:, hh:hh + 1])
        tiles.append(jnp.where(low_half, pair[0], pair[1]))
    a = u * jnp.concatenate(tiles, axis=1)
    o_ref[0] = (a * _rms_scale(a) * onw_ref[...]).astype(BF16)


def _gmlp(uv3, nw, ws, bst, onw):
    b, s, _ = uv3.shape
    const2 = lambda bi, ci: (0, 0)
    return pl.pallas_call(
        _gmlp_kernel,
        grid=(b, s // CHUNK),
        in_specs=[
            pl.BlockSpec((1, CHUNK, D_A), lambda bi, ci: (bi, ci, 0)),
            pl.BlockSpec((1, CHUNK, D_A), lambda bi, ci: (bi, ci, 1)),
            pl.BlockSpec((1, D_A), const2),
            pl.BlockSpec((H_A, CHUNK, CHUNK), lambda bi, ci: (0, 0, 0)),
            pl.BlockSpec((CHUNK, LANES), const2),
            pl.BlockSpec((1, D_A), const2),
        ],
        out_specs=pl.BlockSpec((1, CHUNK, D_A), lambda bi, ci: (bi, ci, 0)),
        out_shape=jax.ShapeDtypeStruct((b, s, D_A), BF16),
        compiler_params=pltpu.CompilerParams(
            dimension_semantics=("parallel", "parallel"), vmem_limit_bytes=VMEM_LIMIT_BYTES),
        name="gmlp",
    )(uv3, uv3, nw, ws, bst, onw)


def _compress_kernel(g_ref, pos_ref, w1_ref, w2_ref, o_ref):
    half = CMP_STRIDE * HEAD_DIM
    g = g_ref[0, 0, 0].astype(F32)
    ga = (g + pos_ref[0, 0:1, :]).astype(BF16)
    gb = (g + pos_ref[0, 1:2, :]).astype(BF16)
    h1 = _dot(ga, w1_ref[0, :half, :])
    h2 = _dot(gb, w1_ref[0, half:, :])
    n_grp = h2.shape[0]
    h2_next = pltpu.roll(h2, n_grp - 1, 0)
    hid = _gelu_tanh(h1 + h2_next).astype(BF16)
    o_ref[0, 0, 0] = _dot(hid, w2_ref[0]).astype(BF16)


def _compress(groups, pos, w1, w2):
    b, _, _, n_grp, width = groups.shape
    return pl.pallas_call(
        _compress_kernel,
        grid=(b, 2, N_KV),
        in_specs=[
            pl.BlockSpec((1, 1, 1, n_grp, width), lambda bi, ti, hi: (bi, ti, hi, 0, 0)),
            pl.BlockSpec((1, 2, width), lambda bi, ti, hi: (ti, 0, 0)),
            pl.BlockSpec((1, 2 * width, CMP_HIDDEN), lambda bi, ti, hi: (ti, 0, 0)),
            pl.BlockSpec((1, CMP_HIDDEN, HEAD_DIM), lambda bi, ti, hi: (ti, 0, 0)),
        ],
        out_specs=pl.BlockSpec((1, 1, 1, n_grp, HEAD_DIM), lambda bi, ti, hi: (bi, ti, hi, 0, 0)),
        out_shape=jax.ShapeDtypeStruct((b, 2, N_KV, n_grp, HEAD_DIM), BF16),
        compiler_params=pltpu.CompilerParams(
            dimension_semantics=("parallel", "parallel", "parallel"),
            vmem_limit_bytes=VMEM_LIMIT_BYTES),
        name="compress",
    )(groups, pos, w1, w2)


S_BLOCKS = 64


def _softmax_cols(s, mask):
    s = jnp.where(mask, s, NEG)
    p = jnp.exp(s - jnp.max(s, axis=0, keepdims=True))
    return p, jnp.sum(p, axis=0, keepdims=True)


def _nsa_kernel(q_ref, kc_ref, vc_ref, ks_ref, vs_ref, kw_ref, vw_ref, g_ref, gb_ref,
                onw_ref, place_ref, o_ref, vs_t_ref, vw_t_ref, vc_t_ref, psum_ref, sel_ref,
                s_a_ref, s_b_ref, p_a_ref, p_b_ref):
    c = pl.program_id(1)
    t0 = c * Q_BLOCK
    n_cmp_pad = kc_ref.shape[1]
    seq = ks_ref.shape[1]

    @pl.when(c == 0)
    def _transpose_values():
        def tr(i, carry):
            r0 = pl.multiple_of(i * LANES, LANES)
            vs_t_ref[i] = vs_ref[0, pl.ds(r0, LANES), :].astype(F32).T.astype(BF16)
            vw_t_ref[i] = vw_ref[0, pl.ds(r0, LANES), :].astype(F32).T.astype(BF16)
            return carry
        lax.fori_loop(0, seq // LANES, tr, 0)
        for i in range(n_cmp_pad // LANES):
            vc_t_ref[i] = vc_ref[0, i * LANES:(i + 1) * LANES, :].astype(F32).T.astype(BF16)

    qp = _dot(q_ref[0], place_ref[...]).astype(BF16)
    q_pad = [qp[:, hh * LANES:(hh + 1) * LANES] for hh in range(H_B)]

    def value_half(v_t, kh):
        return v_t[kh * HEAD_DIM:(kh + 1) * HEAD_DIM]

    n_idx = lax.broadcasted_iota(jnp.int32, (n_cmp_pad, Q_BLOCK), 0)
    tq_c = t0 + lax.broadcasted_iota(jnp.int32, (n_cmp_pad, Q_BLOCK), 1)
    cmp_mask = (n_idx * CMP_STRIDE + (CMP_BLOCK - 1)) <= tq_c
    kc = kc_ref[0]
    vc_t = jnp.concatenate([vc_t_ref[i] for i in range(n_cmp_pad // LANES)], axis=1)
    o_c = []
    p_sum = [None] * N_KV
    scores_c = [_dot_nt(kc, q_pad[hh]) for hh in range(H_B)]
    for hh in range(H_B):
        kh = hh // GQA
        s = jnp.where(cmp_mask, scores_c[hh], NEG)
        m = jnp.max(s, axis=0, keepdims=True)
        p = jnp.where(cmp_mask, jnp.exp(s - m), 0.0)
        l = jnp.sum(p, axis=0, keepdims=True)
        p = p * (1.0 / jnp.where(l == 0.0, 1.0, l))
        o_c.append(_dot(value_half(vc_t, kh), p.astype(BF16)))
        p_sum[kh] = p if p_sum[kh] is None else p_sum[kh] + p

    ratio = SEL_BLOCK // CMP_STRIDE
    span = CMP_BLOCK // CMP_STRIDE
    j_idx = lax.broadcasted_iota(jnp.int32, (SUBLANES, Q_BLOCK), 0)
    cur = (t0 + lax.broadcasted_iota(jnp.int32, (SUBLANES, Q_BLOCK), 1)) // SEL_BLOCK
    n_grp = S_BLOCKS // SUBLANES
    for kh in range(N_KV):
        psum_ref[kh, :SUBLANES, :] = jnp.zeros((SUBLANES, Q_BLOCK), F32)
        psum_ref[kh, SUBLANES:, :] = p_sum[kh]
        imp_t = psum_ref[kh, pl.ds(SUBLANES - (span - 1), S_BLOCKS, stride=ratio), :]
        for k in range(2 - span, ratio):
            imp_t = imp_t + psum_ref[kh, pl.ds(SUBLANES + k, S_BLOCKS, stride=ratio), :]
        score = []
        for r in range(n_grp):
            j = j_idx + r * SUBLANES
            forced = (j == 0) | (j == cur) | (j == cur - 1)
            sc = jnp.where(j <= cur, imp_t[r * SUBLANES:(r + 1) * SUBLANES], -FORCE)
            score.append(jnp.where(forced, FORCE, sc))
        for r in range(n_grp):
            cnt = jnp.zeros((SUBLANES, Q_BLOCK), jnp.int32)
            for jp in range(S_BLOCKS):
                rp = jp // SUBLANES
                other = score[rp][jp % SUBLANES:jp % SUBLANES + 1, :]
                ge = (other >= score[r]).astype(jnp.int32)
                gt = (other > score[r]).astype(jnp.int32)
                if rp < r:
                    ahead = ge
                elif rp > r:
                    ahead = gt
                else:
                    ahead = jnp.where(j_idx > (jp % SUBLANES), ge, gt)
                cnt = cnt + ahead
            sel_ref[kh, r * SUBLANES:(r + 1) * SUBLANES, :] = jnp.where(cnt < N_SELECT, 1.0, 0.0)

    key_row = lax.broadcasted_iota(jnp.int32, (SEL_TILE, Q_BLOCK), 0)
    tq = t0 + lax.broadcasted_iota(jnp.int32, (SEL_TILE, Q_BLOCK), 1)
    chunks = SEL_TILE // LANES
    blocks = SEL_TILE // SEL_BLOCK

    last_tile = seq // SEL_TILE - 1

    def score_tile(tile, s_ref):
        k0 = pl.multiple_of(tile * SEL_TILE, SEL_TILE)
        kt = ks_ref[0, pl.ds(k0, SEL_TILE), :]
        for hh in range(H_B):
            s_ref[hh] = _dot_nt(kt, q_pad[hh])

    def value_products(tile, p_ref):
        v_t = jnp.concatenate([vs_t_ref[tile * chunks + j] for j in range(chunks)], axis=1)
        return [_dot(value_half(v_t, hh // GQA), p_ref[hh]) for hh in range(H_B)]

    def stage(tile, s_cur, s_next, p_cur, p_prev, states):
        pv_prev = value_products(jnp.maximum(tile - 1, 0), p_prev)
        score_tile(jnp.minimum(tile + 1, last_tile), s_next)
        causal = (tile * SEL_TILE + key_row) <= tq
        masks = []
        for kh in range(N_KV):
            ind = jnp.concatenate(
                [jnp.broadcast_to(sel_ref[kh, pl.ds(tile * blocks + j, 1), :],
                                  (SEL_BLOCK, Q_BLOCK)) for j in range(blocks)], axis=0)
            masks.append(jnp.where(causal, ind, 0.0) > 0.5)
        out = []
        for hh in range(H_B):
            m, l, alpha_prev, acc = states[hh]
            s = jnp.where(masks[hh // GQA], s_cur[hh], NEG)
            m_new = jnp.maximum(m, jnp.max(s, axis=0, keepdims=True))
            alpha = jnp.exp(m - m_new)
            p = jnp.exp(s - m_new)
            p_cur[hh] = p.astype(BF16)
            out.append((m_new, alpha * l + jnp.sum(p, axis=0, keepdims=True), alpha,
                        alpha_prev * acc + pv_prev[hh]))
        return tuple(out)

    def pair_body(j, states):
        states = stage(2 * j, s_a_ref, s_b_ref, p_a_ref, p_b_ref, states)
        return stage(2 * j + 1, s_b_ref, s_a_ref, p_b_ref, p_a_ref, states)

    n_pairs = (t0 + Q_BLOCK - 1) // (2 * SEL_TILE) + 1
    score_tile(0, s_a_ref)
    p_b_ref[...] = jnp.zeros(p_b_ref.shape, BF16)
    init = (jnp.full((1, Q_BLOCK), NEG, F32), jnp.zeros((1, Q_BLOCK), F32),
            jnp.ones((1, Q_BLOCK), F32), jnp.zeros((HEAD_DIM, Q_BLOCK), F32))
    sel_states = lax.fori_loop(0, n_pairs, pair_body, tuple(init for _ in range(H_B)))
    pv_last = value_products(2 * n_pairs - 1, p_b_ref)
    o_s = [(alpha * acc + pv_last[hh]) * (1.0 / l)
           for hh, (_, l, alpha, acc) in enumerate(sel_states)]

    w0 = pl.multiple_of(jnp.maximum(t0 - WINDOW, 0), Q_BLOCK)
    kw = kw_ref[0, pl.ds(w0, WIN_SPAN), :]
    vw_t = jnp.concatenate([vw_t_ref[w0 // LANES + j] for j in range(WIN_SPAN // LANES)], axis=1)
    diff = (t0 - w0) + (lax.broadcasted_iota(jnp.int32, (WIN_SPAN, Q_BLOCK), 1)
                        - lax.broadcasted_iota(jnp.int32, (WIN_SPAN, Q_BLOCK), 0))
    win_mask = (diff >= 0) & (diff < WINDOW)
    scores_w = [_dot_nt(kw, q_pad[hh]) for hh in range(H_B)]
    o_w = []
    for hh in range(H_B):
        p, l = _softmax_cols(scores_w[hh], win_mask)
        o_w.append(_dot(value_half(vw_t, hh // GQA), p.astype(BF16)) * (1.0 / l))

    gates_t = jax.nn.sigmoid(g_ref[0] + gb_ref[...]).T
    mixed = []
    for hh in range(H_B):
        mixed.append(gates_t[hh:hh + 1] * o_c[hh]
                     + gates_t[H_B + hh:H_B + hh + 1] * o_s[hh]
                     + gates_t[2 * H_B + hh:2 * H_B + hh + 1] * o_w[hh])
    bmix = jnp.concatenate(mixed, axis=0).T
    o_ref[0] = (bmix * _rms_scale(bmix) * onw_ref[...]).astype(BF16)


def _nsa(q3, kcv, kv3, g3, gate_b, onw, place):
    b, s, _ = q3.shape
    n_cmp_pad = kcv.shape[2]
    const2 = lambda bi, ci: (0, 0)
    kv_spec = lambda col: pl.BlockSpec((1, s, D_KV), lambda bi, ci: (bi, 0, col))
    return pl.pallas_call(
        _nsa_kernel,
        grid=(b, s // Q_BLOCK),
        in_specs=[
            pl.BlockSpec((1, Q_BLOCK, D_B), lambda bi, ci: (bi, ci, 0)),
            pl.BlockSpec((None, 1, n_cmp_pad, D_KV), lambda bi, ci: (bi, 0, 0, 0)),
            pl.BlockSpec((None, 1, n_cmp_pad, D_KV), lambda bi, ci: (bi, 1, 0, 0)),
            kv_spec(2), kv_spec(3), kv_spec(4), kv_spec(5),
            pl.BlockSpec((1, Q_BLOCK, LANES), lambda bi, ci: (bi, ci, 0)),
            pl.BlockSpec((1, LANES), const2),
            pl.BlockSpec((1, D_B), const2),
            pl.BlockSpec(place.shape, const2),
        ],
        out_specs=pl.BlockSpec((1, Q_BLOCK, D_B), lambda bi, ci: (bi, ci, 0)),
        out_shape=jax.ShapeDtypeStruct((b, s, D_B), BF16),
        scratch_shapes=[
            pltpu.VMEM((s // LANES, D_KV, LANES), BF16),
            pltpu.VMEM((s // LANES, D_KV, LANES), BF16),
            pltpu.VMEM((n_cmp_pad // LANES, D_KV, LANES), BF16),
            pltpu.VMEM((N_KV, SUBLANES + n_cmp_pad, Q_BLOCK), F32),
            pltpu.VMEM((N_KV, S_BLOCKS, Q_BLOCK), F32),
            pltpu.VMEM((H_B, SEL_TILE, Q_BLOCK), F32),
            pltpu.VMEM((H_B, SEL_TILE, Q_BLOCK), F32),
            pltpu.VMEM((H_B, SEL_TILE, Q_BLOCK), BF16),
            pltpu.VMEM((H_B, SEL_TILE, Q_BLOCK), BF16),
        ],
        compiler_params=pltpu.CompilerParams(
            dimension_semantics=("arbitrary", "arbitrary"), vmem_limit_bytes=VMEM_LIMIT_BYTES),
        name="nsa",
    )(q3, kcv, kcv, kv3, kv3, kv3, kv3, g3, gate_b, onw, place)


def _mix_ffn_kernel(x_ref, a_ref, b_ref, nw_ref, fnw_ref, woa_hbm, wob_hbm, wg_hbm, wu_hbm,
                    wd_hbm, o_ref, woa_ref, wob_ref, wg_ref, wu_ref, wd_ref, sem, *, final_norm):
    @pl.when(pl.program_id(0) == 0)
    def _load_weights():
        pairs = ((woa_hbm, woa_ref), (wob_hbm, wob_ref), (wg_hbm, wg_ref), (wu_hbm, wu_ref),
                 (wd_hbm, wd_ref))
        copies = [pltpu.make_async_copy(src, dst, sem.at[i]) for i, (src, dst) in enumerate(pairs)]
        for cp in copies:
            cp.start()
        for cp in copies:
            cp.wait()

    x1 = x_ref[...] + _dot(a_ref[...], woa_ref[...]) + _dot(b_ref[...], wob_ref[...])
    h = (x1 * _rms_scale(x1) * nw_ref[...]).astype(BF16)
    gate = _dot(h, wg_ref[...])
    up = _dot(h, wu_ref[...])
    act = (gate * jax.nn.sigmoid(gate) * up).astype(BF16)
    acc = x1 + _dot(act, wd_ref[...])
    if final_norm:
        acc = acc * _rms_scale(acc) * fnw_ref[...]
    o_ref[...] = acc


def _mix_ffn(x2d, a2d, b2d, woa, wob, nw, wg, wu, wd, fnw, final_norm):
    t = x2d.shape[0]
    row = lambda i: (i, 0)
    const = lambda i: (0, 0)
    weights = (woa, wob, wg, wu, wd)
    in_hbm = pl.BlockSpec(memory_space=pl.ANY)
    return pl.pallas_call(
        functools.partial(_mix_ffn_kernel, final_norm=final_norm),
        grid=(t // TM_FFN,),
        in_specs=[
            pl.BlockSpec((TM_FFN, D_MODEL), row),
            pl.BlockSpec((TM_FFN, D_A), row),
            pl.BlockSpec((TM_FFN, D_B), row),
            pl.BlockSpec((1, D_MODEL), const),
            pl.BlockSpec((1, D_MODEL), const),
        ] + [in_hbm] * len(weights),
        out_specs=pl.BlockSpec((TM_FFN, D_MODEL), row),
        out_shape=jax.ShapeDtypeStruct((t, D_MODEL), F32),
        scratch_shapes=[pltpu.VMEM(wt.shape, wt.dtype) for wt in weights]
        + [pltpu.SemaphoreType.DMA((len(weights),))],
        compiler_params=pltpu.CompilerParams(
            dimension_semantics=("arbitrary",), vmem_limit_bytes=VMEM_LIMIT_BYTES),
        name="mix_ffn",
    )(x2d, a2d, b2d, nw, fnw, *weights)


def _placement_matrix():
    p = np.zeros((D_B, H_B * LANES), np.float32)
    for hh in range(H_B):
        for d in range(HEAD_DIM):
            p[hh * HEAD_DIM + d, hh * LANES + (hh // GQA) * HEAD_DIM + d] = 1.0
    return jnp.asarray(p, BF16)


def kernel(x, norm_mix_w, w_in, gmlp_norm_w, gmlp_ws, gmlp_bs, cmp_pos_k, cmp_pos_v, cmp_k_w1,
           cmp_k_w2, cmp_v_w1, cmp_v_w2, gate_b, out_norm_a_w, out_norm_b_w, w_o, norm_ffn_w,
           w_gate, w_up, w_down, final_norm_w):
    b, s, d = x.shape
    assert d == D_MODEL and s // SEL_BLOCK == S_BLOCKS and s % Q_BLOCK == 0
    assert (b * s) % TM_PROJ == 0 and (b * s) % TM_FFN == 0
    t = b * s
    n_grp = s // CMP_STRIDE
    half = CMP_STRIDE * HEAD_DIM
    place = _placement_matrix()
    o_uv, o_q, o_kv = 2 * D_A, 2 * D_A + D_B, 2 * D_A + D_B + 6 * D_KV

    x2d = x.reshape(t, d)
    for l in range(DEPTH):
        w = w_in[l]
        wg_pad = jnp.pad(w[:, o_kv:], ((0, 0), (0, LANES - N_GATES * H_B)))
        uv, q, kv, g_raw = _in_proj(
            x2d, norm_mix_w[l][None, :], w[:, :o_uv].astype(BF16), w[:, o_uv:o_q].astype(BF16),
            w[:, o_q:o_kv].astype(BF16), wg_pad.astype(BF16))

        bst = jnp.pad(gmlp_bs[l].T, ((0, 0), (0, LANES - H_A)))
        a_n = _gmlp(uv.reshape(b, s, 2 * D_A), gmlp_norm_w[l][None, :], gmlp_ws[l], bst,
                    out_norm_a_w[l][None, :])

        kv3 = kv.reshape(b, s, 6 * D_KV)
        groups = kv3[:, :, :2 * D_KV].reshape(b, n_grp, CMP_STRIDE, 2, N_KV, HEAD_DIM)
        groups = groups.transpose(0, 3, 4, 1, 2, 5).reshape(b, 2, N_KV, n_grp, half)
        pos = jnp.stack([cmp_pos_k[l].reshape(2, half), cmp_pos_v[l].reshape(2, half)])
        w1 = jnp.stack([cmp_k_w1[l], cmp_v_w1[l]]).astype(BF16)
        w2 = jnp.stack([cmp_k_w2[l], cmp_v_w2[l]]).astype(BF16)
        kcv = _compress(groups, pos, w1, w2)
        kcv = kcv.transpose(0, 1, 3, 2, 4).reshape(b, 2, n_grp, D_KV)

        gb_pad = jnp.pad(gate_b[l], (0, LANES - N_GATES * H_B))[None, :]
        b_n = _nsa(q.reshape(b, s, D_B), kcv, kv3, g_raw.reshape(b, s, LANES), gb_pad,
                   out_norm_b_w[l][None, :], place)

        x2d = _mix_ffn(
            x2d, a_n.reshape(t, D_A), b_n.reshape(t, D_B),
            w_o[l][:D_A].astype(BF16), w_o[l][D_A:].astype(BF16), norm_ffn_w[l][None, :],
            w_gate[l].astype(BF16), w_up[l].astype(BF16), w_down[l].astype(BF16),
            final_norm_w[None, :], final_norm=(l == DEPTH - 1))
    return x2d.reshape(b, s, d)
```

```python
import functools

import jax
import jax.numpy as jnp
import numpy as np
from jax import lax
from jax.experimental import pallas as pl
from jax.experimental.pallas import tpu as pltpu

D_MODEL = 1024
DEPTH = 2
D_A = 512
D_B = 512
HEAD_DIM = 64
H_A = D_A // HEAD_DIM
H_B = D_B // HEAD_DIM
N_KV = 2
GQA = H_B // N_KV
D_KV = N_KV * HEAD_DIM
CHUNK = 128
CMP_BLOCK = 32
CMP_STRIDE = 16
CMP_HIDDEN = 256
SEL_BLOCK = 64
N_SELECT = 16
WINDOW = 512
Q_BLOCK = 128
N_GATES = 3
D_FF = 2816
EPS = 1e-6
NEG = -1e30
FORCE = 1e4
Q_SCALE = HEAD_DIM ** -0.5 * float(np.log2(np.e))

LANES = 128
SUBLANES = 8
VMEM_LIMIT_BYTES = 56 * 1024 * 1024

TM_PROJ = 512
TM_FFN = 512
FF_CHUNK = 256
SEL_TILE = 256
ONES_ROWS = 16
V_ROWS = HEAD_DIM + ONES_ROWS
HEADS_PER_UNIT = 2
N_UNITS = H_B // HEADS_PER_UNIT
WIN_SPAN = WINDOW + Q_BLOCK

F32 = jnp.float32
BF16 = jnp.bfloat16


def _gelu_tanh(x):
    c = np.float32(np.sqrt(2.0 / np.pi))
    return x * (0.5 * (1.0 + jnp.tanh(c * (x + 0.044715 * (x * x * x)))))


def _rms_scale(x):
    return lax.rsqrt(jnp.mean(x * x, axis=-1, keepdims=True) + EPS)


def _dot(a, b):
    return jnp.dot(a, b, preferred_element_type=F32)


def _dot_nt(a, b):
    return lax.dot_general(a, b, (((1,), (1,)), ((), ())), preferred_element_type=F32)


def _in_proj_kernel(x_ref, nw_ref, wuv_ref, wq_ref, wkv_ref, wg_ref,
                    uv_ref, q_ref, kv_ref, g_ref):
    x = x_ref[...]
    h = (x * _rms_scale(x) * nw_ref[...]).astype(BF16)
    uv_ref[...] = _dot(h, wuv_ref[...])
    q_ref[...] = (_dot(h, wq_ref[...]) * Q_SCALE).astype(BF16)
    kv_ref[...] = _dot(h, wkv_ref[...]).astype(BF16)
    g_ref[...] = _dot(h, wg_ref[...])


def _in_proj(x2d, nw, wuv, wq, wkv, wg):
    t = x2d.shape[0]
    const = lambda i: (0, 0)
    row = lambda i: (i, 0)
    return pl.pallas_call(
        _in_proj_kernel,
        grid=(t // TM_PROJ,),
        in_specs=[
            pl.BlockSpec((TM_PROJ, D_MODEL), row),
            pl.BlockSpec((1, D_MODEL), const),
            pl.BlockSpec((D_MODEL, 2 * D_A), const),
            pl.BlockSpec((D_MODEL, D_B), const),
            pl.BlockSpec((D_MODEL, 6 * D_KV), const),
            pl.BlockSpec((D_MODEL, LANES), const),
        ],
        out_specs=[
            pl.BlockSpec((TM_PROJ, 2 * D_A), row),
            pl.BlockSpec((TM_PROJ, D_B), row),
            pl.BlockSpec((TM_PROJ, 6 * D_KV), row),
            pl.BlockSpec((TM_PROJ, LANES), row),
        ],
        out_shape=[
            jax.ShapeDtypeStruct((t, 2 * D_A), F32),
            jax.ShapeDtypeStruct((t, D_B), BF16),
            jax.ShapeDtypeStruct((t, 6 * D_KV), BF16),
            jax.ShapeDtypeStruct((t, LANES), F32),
        ],
        compiler_params=pltpu.CompilerParams(
            dimension_semantics=("parallel",), vmem_limit_bytes=VMEM_LIMIT_BYTES),
        name="in_proj",
    )(x2d, nw, wuv, wq, wkv, wg)


def _gmlp_kernel(u_ref, v_ref, nw_ref, ws_ref, bst_ref, onw_ref, o_ref):
    u = _gelu_tanh(u_ref[0])
    v = _gelu_tanh(v_ref[0])
    vn = (v * _rms_scale(v) * nw_ref[...]).astype(BF16)
    row = lax.broadcasted_iota(jnp.int32, (CHUNK, CHUNK), 0)
    col = lax.broadcasted_iota(jnp.int32, (CHUNK, CHUNK), 1)
    causal = col <= row
    low_half = col < HEAD_DIM
    bst = bst_ref[...]
    tiles = []
    for i in range(D_A // LANES):
        vt = vn[:, i * LANES:(i + 1) * LANES]
        pair = []
        for hh in (2 * i, 2 * i + 1):
            w = jnp.where(causal, ws_ref[hh], 0.0).astype(BF16)
            pair.append(_dot(w, vt) + bst[:, hh:hh + 1])
        tiles.append(jnp.where(low_half, pair[0], pair[1]))
    a = u * jnp.concatenate(tiles, axis=1)
    o_ref[0] = (a * _rms_scale(a) * onw_ref[...]).astype(BF16)


def _gmlp(uv3, nw, ws, bst, onw):
    b, s, _ = uv3.shape
    const2 = lambda bi, ci: (0, 0)
    return pl.pallas_call(
        _gmlp_kernel,
        grid=(b, s // CHUNK),
        in_specs=[
            pl.BlockSpec((1, CHUNK, D_A), lambda bi, ci: (bi, ci, 0)),
            pl.BlockSpec((1, CHUNK, D_A), lambda bi, ci: (bi, ci, 1)),
            pl.BlockSpec((1, D_A), const2),
            pl.BlockSpec((H_A, CHUNK, CHUNK), lambda bi, ci: (0, 0, 0)),
            pl.BlockSpec((CHUNK, LANES), const2),
            pl.BlockSpec((1, D_A), const2),
        ],
        out_specs=pl.BlockSpec((1, CHUNK, D_A), lambda bi, ci: (bi, ci, 0)),
        out_shape=jax.ShapeDtypeStruct((b, s, D_A), BF16),
        compiler_params=pltpu.CompilerParams(
            dimension_semantics=("parallel", "parallel"), vmem_limit_bytes=VMEM_LIMIT_BYTES),
        name="gmlp",
    )(uv3, uv3, nw, ws, bst, onw)


def _compress_kernel(g_ref, pos_ref, w1_ref, w2_ref, o_ref):
    half = CMP_STRIDE * HEAD_DIM
    g = g_ref[0, 0, 0].astype(F32)
    ga = (g + pos_ref[0, 0:1, :]).astype(BF16)
    gb = (g + pos_ref[0, 1:2, :]).astype(BF16)
    h1 = _dot(ga, w1_ref[0, :half, :])
    h2 = _dot(gb, w1_ref[0, half:, :])
    n_grp = h2.shape[0]
    h2_next = pltpu.roll(h2, n_grp - 1, 0)
    hid = _gelu_tanh(h1 + h2_next).astype(BF16)
    o_ref[0, 0, 0] = _dot(hid, w2_ref[0]).astype(BF16)


def _compress(groups, pos, w1, w2):
    b, _, _, n_grp, width = groups.shape
    return pl.pallas_call(
        _compress_kernel,
        grid=(b, 2, N_KV),
        in_specs=[
            pl.BlockSpec((1, 1, 1, n_grp, width), lambda bi, ti, hi: (bi, ti, hi, 0, 0)),
            pl.BlockSpec((1, 2, width), lambda bi, ti, hi: (ti, 0, 0)),
            pl.BlockSpec((1, 2 * width, CMP_HIDDEN), lambda bi, ti, hi: (ti, 0, 0)),
            pl.BlockSpec((1, CMP_HIDDEN, HEAD_DIM), lambda bi, ti, hi: (ti, 0, 0)),
        ],
        out_specs=pl.BlockSpec((1, 1, 1, n_grp, HEAD_DIM), lambda bi, ti, hi: (bi, ti, hi, 0, 0)),
        out_shape=jax.ShapeDtypeStruct((b, 2, N_KV, n_grp, HEAD_DIM), BF16),
        compiler_params=pltpu.CompilerParams(
            dimension_semantics=("parallel", "parallel", "parallel"),
            vmem_limit_bytes=VMEM_LIMIT_BYTES),
        name="compress",
    )(groups, pos, w1, w2)


S_BLOCKS = 64


def _softmax_cols(s, mask):
    s = jnp.where(mask, s, NEG)
    return jnp.exp2(s - jnp.max(s, axis=0, keepdims=True)).astype(BF16)


def _own_lanes(lane, kh):
    return (lane >= kh * HEAD_DIM) & (lane < (kh + 1) * HEAD_DIM)


def _spare_lane0(kh):
    return (1 - kh) * HEAD_DIM


def _normalize(r):
    return r[:HEAD_DIM] * (1.0 / r[HEAD_DIM:HEAD_DIM + 1])


def _nsa_kernel(q_ref, kc_ref, vc_ref, ks_ref, vs_ref, kw_ref, vw_ref, g_ref, gb_ref,
                onw_ref, place_ref, o_ref, ks_aug_ref, vs_t_ref, vw_t_ref, vc_t_ref, psum_ref,
                s_a_ref, s_b_ref, p_a_ref, p_b_ref):
    c = pl.program_id(1)
    t0 = c * Q_BLOCK
    n_cmp_pad = kc_ref.shape[1]
    seq = ks_ref.shape[1]
    unit_w = HEADS_PER_UNIT * Q_BLOCK

    @pl.when(c == 0)
    def _prepare_batch_row():
        lane = lax.broadcasted_iota(jnp.int32, (LANES, LANES), 1)
        blk_of_row = lax.broadcasted_iota(jnp.int32, (LANES, LANES), 0) // SEL_BLOCK

        def prep(i, carry):
            r0 = pl.multiple_of(i * LANES, LANES)
            keys = ks_ref[0, pl.ds(r0, LANES), :]
            key_blk = blk_of_row + i * (LANES // SEL_BLOCK)
            for kh in range(N_KV):
                one_hot = jnp.where(lane - _spare_lane0(kh) == key_blk, 1.0, 0.0).astype(BF16)
                ks_aug_ref[kh, pl.ds(r0, LANES), :] = jnp.where(_own_lanes(lane, kh), keys, one_hot)
            for src, dst in ((vs_ref, vs_t_ref), (vw_ref, vw_t_ref)):
                chunk_t = src[0, pl.ds(r0, LANES), :].astype(F32).T.astype(BF16)
                for kh in range(N_KV):
                    dst[i, kh, :HEAD_DIM] = chunk_t[kh * HEAD_DIM:(kh + 1) * HEAD_DIM]
                    dst[i, kh, HEAD_DIM:] = jnp.ones((ONES_ROWS, LANES), BF16)
            return carry
        lax.fori_loop(0, seq // LANES, prep, 0)
        for i in range(n_cmp_pad // LANES):
            vc_t_ref[i] = vc_ref[0, i * LANES:(i + 1) * LANES, :].astype(F32).T.astype(BF16)

    qp = _dot(q_ref[0], place_ref[...]).astype(BF16)
    q_unit = [jnp.concatenate([qp[:, hh * LANES:(hh + 1) * LANES]
                               for hh in range(u * HEADS_PER_UNIT, (u + 1) * HEADS_PER_UNIT)],
                              axis=0) for u in range(N_UNITS)]
    unit_kv = [u * HEADS_PER_UNIT // GQA for u in range(N_UNITS)]

    def q_pos(rows):
        return t0 + (lax.broadcasted_iota(jnp.int32, (rows, unit_w), 1) & (Q_BLOCK - 1))

    n_idx = lax.broadcasted_iota(jnp.int32, (n_cmp_pad, unit_w), 0)
    cmp_mask = (n_idx * CMP_STRIDE + (CMP_BLOCK - 1)) <= q_pos(n_cmp_pad)
    kc = kc_ref[0]
    vc_t = jnp.concatenate([vc_t_ref[i] for i in range(n_cmp_pad // LANES)], axis=1)
    o_c = []
    p_sum = [None] * N_KV
    scores_c = [_dot_nt(kc, q_unit[u]) for u in range(N_UNITS)]
    for u in range(N_UNITS):
        kh = unit_kv[u]
        s = jnp.where(cmp_mask, scores_c[u], NEG)
        m = jnp.max(s, axis=0, keepdims=True)
        p = jnp.where(cmp_mask, jnp.exp2(s - m), 0.0)
        l = jnp.sum(p, axis=0, keepdims=True)
        p = p * (1.0 / jnp.where(l == 0.0, 1.0, l))
        o_c.append(_dot(vc_t[kh * HEAD_DIM:(kh + 1) * HEAD_DIM], p.astype(BF16)))
        for hh in range(HEADS_PER_UNIT):
            ph = p[:, hh * Q_BLOCK:(hh + 1) * Q_BLOCK]
            p_sum[kh] = ph if p_sum[kh] is None else p_sum[kh] + ph

    ratio = SEL_BLOCK // CMP_STRIDE
    span = CMP_BLOCK // CMP_STRIDE
    j_idx = lax.broadcasted_iota(jnp.int32, (SUBLANES, Q_BLOCK), 0)
    cur = (t0 + lax.broadcasted_iota(jnp.int32, (SUBLANES, Q_BLOCK), 1)) // SEL_BLOCK
    first_own_block = t0 // SEL_BLOCK
    n_grp = S_BLOCKS // SUBLANES
    q_aug = [None] * N_UNITS
    for kh in range(N_KV):
        psum_ref[kh, :SUBLANES, :] = jnp.zeros((SUBLANES, Q_BLOCK), F32)
        psum_ref[kh, SUBLANES:, :] = p_sum[kh]
        imp_t = psum_ref[kh, pl.ds(SUBLANES - (span - 1), S_BLOCKS, stride=ratio), :]
        for k in range(2 - span, ratio):
            imp_t = imp_t + psum_ref[kh, pl.ds(SUBLANES + k, S_BLOCKS, stride=ratio), :]
        score = []
        for r in range(n_grp):
            j = j_idx + r * SUBLANES
            forced = (j == 0) | (j == cur) | (j == cur - 1)
            sc = jnp.where(j <= cur, imp_t[r * SUBLANES:(r + 1) * SUBLANES], -FORCE)
            score.append(jnp.where(forced, FORCE, sc))
        bias_rows = []
        for r in range(n_grp):
            cnt = jnp.zeros((SUBLANES, Q_BLOCK), jnp.int32)
            for jp in range(S_BLOCKS):
                rp = jp // SUBLANES
                other = score[rp][jp % SUBLANES:jp % SUBLANES + 1, :]
                ge = (other >= score[r]).astype(jnp.int32)
                gt = (other > score[r]).astype(jnp.int32)
                if rp < r:
                    ahead = ge
                elif rp > r:
                    ahead = gt
                else:
                    ahead = jnp.where(j_idx > (jp % SUBLANES), ge, gt)
                cnt = cnt + ahead
            past = (j_idx + r * SUBLANES) < first_own_block
            bias_rows.append(jnp.where(past, jnp.where(cnt < N_SELECT, 0.0, NEG), NEG))
        zero_rows = [jnp.zeros((LANES - S_BLOCKS, Q_BLOCK), F32)]
        bias_t = jnp.concatenate(
            zero_rows + bias_rows if _spare_lane0(kh) else bias_rows + zero_rows, axis=0)
        bias_q = jnp.concatenate([bias_t.T.astype(BF16)] * HEADS_PER_UNIT, axis=0)
        own = _own_lanes(lax.broadcasted_iota(jnp.int32, bias_q.shape, 1), kh)
        for u in range(N_UNITS):
            if unit_kv[u] == kh:
                q_aug[u] = jnp.where(own, q_unit[u], bias_q)

    def value_products(v_ref, chunk0, n_chunks, p_of_unit):
        v_aug = [jnp.concatenate([v_ref[chunk0 + j, kh] for j in range(n_chunks)], axis=1)
                 if n_chunks > 1 else v_ref[chunk0, kh] for kh in range(N_KV)]
        return [_dot(v_aug[unit_kv[u]], p_of_unit(u)) for u in range(N_UNITS)]

    w0 = pl.multiple_of(jnp.maximum(t0 - WINDOW, 0), Q_BLOCK)
    kw = kw_ref[0, pl.ds(w0, WIN_SPAN), :]
    diff = q_pos(WIN_SPAN) - (w0 + lax.broadcasted_iota(jnp.int32, (WIN_SPAN, unit_w), 0))
    win_mask = (diff >= 0) & (diff < WINDOW)
    scores_w = [_dot_nt(kw, q_unit[u]) for u in range(N_UNITS)]
    p_w = [_softmax_cols(scores_w[u], win_mask) for u in range(N_UNITS)]
    o_w = [_normalize(r) for r in
           value_products(vw_t_ref, w0 // LANES, WIN_SPAN // LANES, lambda u: p_w[u])]

    key_d = lax.broadcasted_iota(jnp.int32, (Q_BLOCK, unit_w), 0)
    causal_d = (t0 + key_d) <= q_pos(Q_BLOCK)
    kd = ks_ref[0, pl.ds(pl.multiple_of(t0, Q_BLOCK), Q_BLOCK), :]
    scores_d = [_dot_nt(kd, q_unit[u]) for u in range(N_UNITS)]
    m_d, p_d = [], []
    for u in range(N_UNITS):
        s = jnp.where(causal_d, scores_d[u], NEG)
        m_d.append(jnp.max(s, axis=0, keepdims=True))
        p_d.append(jnp.exp2(s - m_d[u]).astype(BF16))
    acc_d = value_products(vs_t_ref, c, 1, lambda u: p_d[u])

    chunks = SEL_TILE // LANES
    last_tile = seq // SEL_TILE - 1

    def score_tile(tile, s_ref):
        k0 = pl.multiple_of(tile * SEL_TILE, SEL_TILE)
        kt = [ks_aug_ref[kh, pl.ds(k0, SEL_TILE), :] for kh in range(N_KV)]
        for u in range(N_UNITS):
            s_ref[u] = _dot_nt(kt[unit_kv[u]], q_aug[u])

    def stage(tile, s_cur, s_next, p_cur, p_prev, states):
        pv_prev = value_products(vs_t_ref, jnp.maximum(tile - 1, 0) * chunks, chunks,
                                 lambda u: p_prev[u])
        score_tile(jnp.minimum(tile + 1, last_tile), s_next)
        out = []
        for u in range(N_UNITS):
            m, alpha_prev, acc = states[u]
            s = s_cur[u]
            m_new = jnp.maximum(m, jnp.max(s, axis=0, keepdims=True))
            p_cur[u] = jnp.exp2(s - m_new).astype(BF16)
            out.append((m_new, jnp.exp2(m - m_new), alpha_prev * acc + pv_prev[u]))
        return tuple(out)

    def pair_body(j, states):
        states = stage(2 * j, s_a_ref, s_b_ref, p_a_ref, p_b_ref, states)
        return stage(2 * j + 1, s_b_ref, s_a_ref, p_b_ref, p_a_ref, states)

    blocks_per_pair = 2 * SEL_TILE // Q_BLOCK
    n_pairs = (c + blocks_per_pair - 1) // blocks_per_pair
    score_tile(0, s_a_ref)
    p_b_ref[...] = jnp.zeros(p_b_ref.shape, BF16)
    init = tuple((m_d[u], jnp.ones((1, unit_w), F32), acc_d[u]) for u in range(N_UNITS))
    sel_states = lax.fori_loop(0, n_pairs, pair_body, init)
    pv_last = value_products(vs_t_ref, jnp.maximum(2 * n_pairs - 1, 0) * chunks, chunks,
                             lambda u: p_b_ref[u])
    o_s = [_normalize(alpha * acc + pv_last[u]) for u, (_, alpha, acc) in enumerate(sel_states)]

    gates_t = jax.nn.sigmoid(g_ref[0] + gb_ref[...]).T
    mixed = []
    for hh in range(H_B):
        u, sl = hh // HEADS_PER_UNIT, slice((hh % HEADS_PER_UNIT) * Q_BLOCK,
                                            (hh % HEADS_PER_UNIT + 1) * Q_BLOCK)
        mixed.append(gates_t[hh:hh + 1] * o_c[u][:, sl]
                     + gates_t[H_B + hh:H_B + hh + 1] * o_s[u][:, sl]
                     + gates_t[2 * H_B + hh:2 * H_B + hh + 1] * o_w[u][:, sl])
    bmix = jnp.concatenate(mixed, axis=0).T
    o_ref[0] = (bmix * _rms_scale(bmix) * onw_ref[...]).astype(BF16)


def _nsa(q3, kcv, kv3, g3, gate_b, onw, place):
    b, s, _ = q3.shape
    n_cmp_pad = kcv.shape[2]
    unit_w = HEADS_PER_UNIT * Q_BLOCK
    const2 = lambda bi, ci: (0, 0)
    kv_spec = lambda col: pl.BlockSpec((1, s, D_KV), lambda bi, ci: (bi, 0, col))
    return pl.pallas_call(
        _nsa_kernel,
        grid=(b, s // Q_BLOCK),
        in_specs=[
            pl.BlockSpec((1, Q_BLOCK, D_B), lambda bi, ci: (bi, ci, 0)),
            pl.BlockSpec((None, 1, n_cmp_pad, D_KV), lambda bi, ci: (bi, 0, 0, 0)),
            pl.BlockSpec((None, 1, n_cmp_pad, D_KV), lambda bi, ci: (bi, 1, 0, 0)),
            kv_spec(2), kv_spec(3), kv_spec(4), kv_spec(5),
            pl.BlockSpec((1, Q_BLOCK, LANES), lambda bi, ci: (bi, ci, 0)),
            pl.BlockSpec((1, LANES), const2),
            pl.BlockSpec((1, D_B), const2),
            pl.BlockSpec(place.shape, const2),
        ],
        out_specs=pl.BlockSpec((1, Q_BLOCK, D_B), lambda bi, ci: (bi, ci, 0)),
        out_shape=jax.ShapeDtypeStruct((b, s, D_B), BF16),
        scratch_shapes=[
            pltpu.VMEM((N_KV, s, LANES), BF16),
            pltpu.VMEM((s // LANES, N_KV, V_ROWS, LANES), BF16),
            pltpu.VMEM((s // LANES, N_KV, V_ROWS, LANES), BF16),
            pltpu.VMEM((n_cmp_pad // LANES, D_KV, LANES), BF16),
            pltpu.VMEM((N_KV, SUBLANES + n_cmp_pad, Q_BLOCK), F32),
            pltpu.VMEM((N_UNITS, SEL_TILE, unit_w), F32),
            pltpu.VMEM((N_UNITS, SEL_TILE, unit_w), F32),
            pltpu.VMEM((N_UNITS, SEL_TILE, unit_w), BF16),
            pltpu.VMEM((N_UNITS, SEL_TILE, unit_w), BF16),
        ],
        compiler_params=pltpu.CompilerParams(
            dimension_semantics=("arbitrary", "arbitrary"), vmem_limit_bytes=VMEM_LIMIT_BYTES),
        name="nsa",
    )(q3, kcv, kcv, kv3, kv3, kv3, kv3, g3, gate_b, onw, place)


def _mix_ffn_kernel(x_ref, a_ref, b_ref, nw_ref, fnw_ref, woa_hbm, wob_hbm, wg_hbm, wu_hbm,
                    wd_hbm, o_ref, woa_ref, wob_ref, wg_ref, wu_ref, wd_ref, sem, *, final_norm):
    @pl.when(pl.program_id(0) == 0)
    def _load_weights():
        pairs = ((woa_hbm, woa_ref), (wob_hbm, wob_ref), (wg_hbm, wg_ref), (wu_hbm, wu_ref),
                 (wd_hbm, wd_ref))
        copies = [pltpu.make_async_copy(src, dst, sem.at[i]) for i, (src, dst) in enumerate(pairs)]
        for cp in copies:
            cp.start()
        for cp in copies:
            cp.wait()

    x1 = x_ref[...] + _dot(a_ref[...], woa_ref[...]) + _dot(b_ref[...], wob_ref[...])
    h = (x1 * _rms_scale(x1) * nw_ref[...]).astype(BF16)
    gate = _dot(h, wg_ref[...])
    up = _dot(h, wu_ref[...])
    act = (gate * jax.nn.sigmoid(gate) * up).astype(BF16)
    acc = x1 + _dot(act, wd_ref[...])
    if final_norm:
        acc = acc * _rms_scale(acc) * fnw_ref[...]
    o_ref[...] = acc


def _mix_ffn(x2d, a2d, b2d, woa, wob, nw, wg, wu, wd, fnw, final_norm):
    t = x2d.shape[0]
    row = lambda i: (i, 0)
    const = lambda i: (0, 0)
    weights = (woa, wob, wg, wu, wd)
    in_hbm = pl.BlockSpec(memory_space=pl.ANY)
    return pl.pallas_call(
        functools.partial(_mix_ffn_kernel, final_norm=final_norm),
        grid=(t // TM_FFN,),
        in_specs=[
            pl.BlockSpec((TM_FFN, D_MODEL), row),
            pl.BlockSpec((TM_FFN, D_A), row),
            pl.BlockSpec((TM_FFN, D_B), row),
            pl.BlockSpec((1, D_MODEL), const),
            pl.BlockSpec((1, D_MODEL), const),
        ] + [in_hbm] * len(weights),
        out_specs=pl.BlockSpec((TM_FFN, D_MODEL), row),
        out_shape=jax.ShapeDtypeStruct((t, D_MODEL), F32),
        scratch_shapes=[pltpu.VMEM(wt.shape, wt.dtype) for wt in weights]
        + [pltpu.SemaphoreType.DMA((len(weights),))],
        compiler_params=pltpu.CompilerParams(
            dimension_semantics=("arbitrary",), vmem_limit_bytes=VMEM_LIMIT_BYTES),
        name="mix_ffn",
    )(x2d, a2d, b2d, nw, fnw, *weights)


def _placement_matrix():
    p = np.zeros((D_B, H_B * LANES), np.float32)
    for hh in range(H_B):
        for d in range(HEAD_DIM):
            p[hh * HEAD_DIM + d, hh * LANES + (hh // GQA) * HEAD_DIM + d] = 1.0
    return jnp.asarray(p, BF16)


def kernel(x, norm_mix_w, w_in, gmlp_norm_w, gmlp_ws, gmlp_bs, cmp_pos_k, cmp_pos_v, cmp_k_w1,
           cmp_k_w2, cmp_v_w1, cmp_v_w2, gate_b, out_norm_a_w, out_norm_b_w, w_o, norm_ffn_w,
           w_gate, w_up, w_down, final_norm_w):
    b, s, d = x.shape
    assert d == D_MODEL and s // SEL_BLOCK == S_BLOCKS and s % Q_BLOCK == 0
    assert (b * s) % TM_PROJ == 0 and (b * s) % TM_FFN == 0
    t = b * s
    n_grp = s // CMP_STRIDE
    half = CMP_STRIDE * HEAD_DIM
    place = _placement_matrix()
    o_uv, o_q, o_kv = 2 * D_A, 2 * D_A + D_B, 2 * D_A + D_B + 6 * D_KV

    x2d = x.reshape(t, d)
    for l in range(DEPTH):
        w = w_in[l]
        wg_pad = jnp.pad(w[:, o_kv:], ((0, 0), (0, LANES - N_GATES * H_B)))
        uv, q, kv, g_raw = _in_proj(
            x2d, norm_mix_w[l][None, :], w[:, :o_uv].astype(BF16), w[:, o_uv:o_q].astype(BF16),
            w[:, o_q:o_kv].astype(BF16), wg_pad.astype(BF16))

        bst = jnp.pad(gmlp_bs[l].T, ((0, 0), (0, LANES - H_A)))
        a_n = _gmlp(uv.reshape(b, s, 2 * D_A), gmlp_norm_w[l][None, :], gmlp_ws[l], bst,
                    out_norm_a_w[l][None, :])

        kv3 = kv.reshape(b, s, 6 * D_KV)
        groups = kv3[:, :, :2 * D_KV].reshape(b, n_grp, CMP_STRIDE, 2, N_KV, HEAD_DIM)
        groups = groups.transpose(0, 3, 4, 1, 2, 5).reshape(b, 2, N_KV, n_grp, half)
        pos = jnp.stack([cmp_pos_k[l].reshape(2, half), cmp_pos_v[l].reshape(2, half)])
        w1 = jnp.stack([cmp_k_w1[l], cmp_v_w1[l]]).astype(BF16)
        w2 = jnp.stack([cmp_k_w2[l], cmp_v_w2[l]]).astype(BF16)
        kcv = _compress(groups, pos, w1, w2)
        kcv = kcv.transpose(0, 1, 3, 2, 4).reshape(b, 2, n_grp, D_KV)

        gb_pad = jnp.pad(gate_b[l], (0, LANES - N_GATES * H_B))[None, :]
        b_n = _nsa(q.reshape(b, s, D_B), kcv, kv3, g_raw.reshape(b, s, LANES), gb_pad,
                   out_norm_b_w[l][None, :], place)

        x2d = _mix_ffn(
            x2d, a_n.reshape(t, D_A), b_n.reshape(t, D_B),
            w_o[l][:D_A].astype(BF16), w_o[l][D_A:].astype(BF16), norm_ffn_w[l][None, :],
            w_gate[l].astype(BF16), w_up[l].astype(BF16), w_down[l].astype(BF16),
            final_norm_w[None, :], final_norm=(l == DEPTH - 1))
    return x2d.reshape(b, s, d)
```

```python
import functools

import jax
import jax.numpy as jnp
import numpy as np
from jax import lax
from jax.experimental import pallas as pl
from jax.experimental.pallas import tpu as pltpu

D_MODEL = 1024
DEPTH = 2
D_A = 512
D_B = 512
HEAD_DIM = 64
H_A = D_A // HEAD_DIM
H_B = D_B // HEAD_DIM
N_KV = 2
GQA = H_B // N_KV
D_KV = N_KV * HEAD_DIM
CHUNK = 128
CMP_BLOCK = 32
CMP_STRIDE = 16
CMP_HIDDEN = 256
SEL_BLOCK = 64
N_SELECT = 16
WINDOW = 512
Q_BLOCK = 128
N_GATES = 3
D_FF = 2816
EPS = 1e-6
NEG = -1e30
FORCE = 1e4
Q_SCALE = HEAD_DIM ** -0.5 * float(np.log2(np.e))

LANES = 128
SUBLANES = 8
VMEM_LIMIT_BYTES = 56 * 1024 * 1024

TM_PROJ = 512
TM_FFN = 512
GMLP_CHUNKS = 4
FF_CHUNK = 256
SEL_TILE = 256
ONES_ROWS = 16
V_ROWS = HEAD_DIM + ONES_ROWS
HEADS_PER_UNIT = 2
N_UNITS = H_B // HEADS_PER_UNIT
WIN_SPAN = WINDOW + Q_BLOCK

F32 = jnp.float32
BF16 = jnp.bfloat16


def _gelu_tanh(x):
    c = np.float32(np.sqrt(2.0 / np.pi))
    return x * (0.5 * (1.0 + jnp.tanh(c * (x + 0.044715 * (x * x * x)))))


def _rms_scale(x):
    return lax.rsqrt(jnp.mean(x * x, axis=-1, keepdims=True) + EPS)


def _dot(a, b):
    return jnp.dot(a, b, preferred_element_type=F32)


def _dot_nt(a, b):
    return lax.dot_general(a, b, (((1,), (1,)), ((), ())), preferred_element_type=F32)


def _in_proj_kernel(x_ref, nw_ref, wuv_ref, wq_ref, wc_ref, wkv_ref, wg_ref,
                    uv_ref, q_ref, c_ref, kv_ref, g_ref):
    x = x_ref[...]
    h = (x * _rms_scale(x) * nw_ref[...]).astype(BF16)
    uv_ref[...] = _dot(h, wuv_ref[...])
    q_ref[...] = (_dot(h, wq_ref[...]) * Q_SCALE).astype(BF16)
    c_ref[...] = _dot(h, wc_ref[...])
    kv_ref[...] = _dot(h, wkv_ref[...]).astype(BF16)
    g_ref[...] = _dot(h, wg_ref[...])


def _in_proj(x2d, nw, wuv, wq, wc, wkv, wg):
    t = x2d.shape[0]
    const = lambda i: (0, 0)
    row = lambda i: (i, 0)
    return pl.pallas_call(
        _in_proj_kernel,
        grid=(t // TM_PROJ,),
        in_specs=[
            pl.BlockSpec((TM_PROJ, D_MODEL), row),
            pl.BlockSpec((1, D_MODEL), const),
            pl.BlockSpec((D_MODEL, 2 * D_A), const),
            pl.BlockSpec((D_MODEL, D_B), const),
            pl.BlockSpec((D_MODEL, 2 * D_KV), const),
            pl.BlockSpec((D_MODEL, 4 * D_KV), const),
            pl.BlockSpec((D_MODEL, LANES), const),
        ],
        out_specs=[
            pl.BlockSpec((TM_PROJ, 2 * D_A), row),
            pl.BlockSpec((TM_PROJ, D_B), row),
            pl.BlockSpec((TM_PROJ, 2 * D_KV), row),
            pl.BlockSpec((TM_PROJ, 4 * D_KV), row),
            pl.BlockSpec((TM_PROJ, LANES), row),
        ],
        out_shape=[
            jax.ShapeDtypeStruct((t, 2 * D_A), F32),
            jax.ShapeDtypeStruct((t, D_B), BF16),
            jax.ShapeDtypeStruct((t, 2 * D_KV), F32),
            jax.ShapeDtypeStruct((t, 4 * D_KV), BF16),
            jax.ShapeDtypeStruct((t, LANES), F32),
        ],
        compiler_params=pltpu.CompilerParams(
            dimension_semantics=("parallel",), vmem_limit_bytes=VMEM_LIMIT_BYTES),
        name="in_proj",
    )(x2d, nw, wuv, wq, wc, wkv, wg)


def _gmlp_kernel(u_ref, v_ref, nw_ref, ws_ref, bst_ref, onw_ref, o_ref):
    row = lax.broadcasted_iota(jnp.int32, (CHUNK, CHUNK), 0)
    col = lax.broadcasted_iota(jnp.int32, (CHUNK, CHUNK), 1)
    causal = col <= row
    low_half = col < HEAD_DIM
    bst = bst_ref[...]
    w = [jnp.where(causal, ws_ref[hh], 0.0).astype(BF16) for hh in range(H_A)]
    for ci in range(GMLP_CHUNKS):
        rows = slice(ci * CHUNK, (ci + 1) * CHUNK)
        u = _gelu_tanh(u_ref[0, rows, :])
        v = _gelu_tanh(v_ref[0, rows, :])
        vn = (v * _rms_scale(v) * nw_ref[...]).astype(BF16)
        tiles = []
        for i in range(D_A // LANES):
            vt = vn[:, i * LANES:(i + 1) * LANES]
            pair = [_dot(w[hh], vt) + bst[:, hh:hh + 1] for hh in (2 * i, 2 * i + 1)]
            tiles.append(jnp.where(low_half, pair[0], pair[1]))
        a = u * jnp.concatenate(tiles, axis=1)
        o_ref[0, rows, :] = (a * _rms_scale(a) * onw_ref[...]).astype(BF16)


def _gmlp(uv3, nw, ws, bst, onw):
    b, s, _ = uv3.shape
    rows = GMLP_CHUNKS * CHUNK
    const2 = lambda bi, ci: (0, 0)
    return pl.pallas_call(
        _gmlp_kernel,
        grid=(b, s // rows),
        in_specs=[
            pl.BlockSpec((1, rows, D_A), lambda bi, ci: (bi, ci, 0)),
            pl.BlockSpec((1, rows, D_A), lambda bi, ci: (bi, ci, 1)),
            pl.BlockSpec((1, D_A), const2),
            pl.BlockSpec((H_A, CHUNK, CHUNK), lambda bi, ci: (0, 0, 0)),
            pl.BlockSpec((CHUNK, LANES), const2),
            pl.BlockSpec((1, D_A), const2),
        ],
        out_specs=pl.BlockSpec((1, rows, D_A), lambda bi, ci: (bi, ci, 0)),
        out_shape=jax.ShapeDtypeStruct((b, s, D_A), BF16),
        compiler_params=pltpu.CompilerParams(
            dimension_semantics=("parallel", "parallel"), vmem_limit_bytes=VMEM_LIMIT_BYTES),
        name="gmlp",
    )(uv3, uv3, nw, ws, bst, onw)


def _compress_kernel(x_ref, pos_ref, w1_ref, w2_ref, o_ref, h1_ref, h2_ref):
    n_grp = o_ref.shape[2]
    for r in range(CMP_STRIDE):
        xr = x_ref[0, pl.ds(r, n_grp, stride=CMP_STRIDE), :]
        lo = _dot((xr + pos_ref[0, r:r + 1, :]).astype(BF16), w1_ref[0, r])
        hi = _dot((xr + pos_ref[0, CMP_STRIDE + r:CMP_STRIDE + r + 1, :]).astype(BF16),
                  w1_ref[0, CMP_STRIDE + r])
        if r == 0:
            h1_ref[...] = lo
            h2_ref[...] = hi
        else:
            h1_ref[...] += lo
            h2_ref[...] += hi
    h2_next = pltpu.roll(h2_ref[...], n_grp - 1, 0)
    hid = _gelu_tanh(h1_ref[...] + h2_next).astype(BF16)
    o_ref[0, 0] = _dot(hid, w2_ref[0]).astype(BF16)


def _compress(x3, pos, w1, w2):
    b, s, _ = x3.shape
    n_grp = s // CMP_STRIDE
    return pl.pallas_call(
        _compress_kernel,
        grid=(b, 2),
        in_specs=[
            pl.BlockSpec((1, s, D_KV), lambda bi, ti: (bi, 0, ti)),
            pl.BlockSpec((1,) + pos.shape[1:], lambda bi, ti: (ti, 0, 0)),
            pl.BlockSpec((1,) + w1.shape[1:], lambda bi, ti: (ti, 0, 0, 0)),
            pl.BlockSpec((1,) + w2.shape[1:], lambda bi, ti: (ti, 0, 0)),
        ],
        out_specs=pl.BlockSpec((1, 1, n_grp, D_KV), lambda bi, ti: (bi, ti, 0, 0)),
        out_shape=jax.ShapeDtypeStruct((b, 2, n_grp, D_KV), BF16),
        scratch_shapes=[pltpu.VMEM((n_grp, N_KV * CMP_HIDDEN), F32)] * 2,
        compiler_params=pltpu.CompilerParams(
            dimension_semantics=("parallel", "parallel"), vmem_limit_bytes=VMEM_LIMIT_BYTES),
        name="compress",
    )(x3, pos, w1, w2)


def _per_kv_head_blockdiag(w):
    k, n = w.shape[-2:]
    out = jnp.zeros(w.shape[:-2] + (N_KV * k, N_KV * n), w.dtype)
    for kh in range(N_KV):
        out = out.at[..., kh * k:(kh + 1) * k, kh * n:(kh + 1) * n].set(w)
    return out


S_BLOCKS = 64


def _softmax_cols(s, mask):
    s = jnp.where(mask, s, NEG)
    return jnp.exp2(s - jnp.max(s, axis=0, keepdims=True)).astype(BF16)


def _own_lanes(lane, kh):
    return (lane >= kh * HEAD_DIM) & (lane < (kh + 1) * HEAD_DIM)


def _spare_lane0(kh):
    return (1 - kh) * HEAD_DIM


def _normalize(r):
    return r[:HEAD_DIM] * (1.0 / r[HEAD_DIM:HEAD_DIM + 1])


def _nsa_kernel(q_ref, kc_ref, vc_ref, ks_ref, vs_ref, kw_ref, vw_ref, g_ref, gb_ref,
                onw_ref, place_ref, o_ref, ks_aug_ref, vs_t_ref, vw_t_ref, vc_t_ref, psum_ref,
                s_a_ref, s_b_ref, p_a_ref, p_b_ref):
    c = pl.program_id(1)
    t0 = c * Q_BLOCK
    n_cmp_pad = kc_ref.shape[1]
    seq = ks_ref.shape[1]
    unit_w = HEADS_PER_UNIT * Q_BLOCK

    @pl.when(c == 0)
    def _prepare_batch_row():
        lane = lax.broadcasted_iota(jnp.int32, (LANES, LANES), 1)
        blk_of_row = lax.broadcasted_iota(jnp.int32, (LANES, LANES), 0) // SEL_BLOCK

        def prep(i, carry):
            r0 = pl.multiple_of(i * LANES, LANES)
            keys = ks_ref[0, pl.ds(r0, LANES), :]
            key_blk = blk_of_row + i * (LANES // SEL_BLOCK)
            for kh in range(N_KV):
                one_hot = jnp.where(lane - _spare_lane0(kh) == key_blk, 1.0, 0.0).astype(BF16)
                ks_aug_ref[kh, pl.ds(r0, LANES), :] = jnp.where(_own_lanes(lane, kh), keys, one_hot)
            for src, dst in ((vs_ref, vs_t_ref), (vw_ref, vw_t_ref)):
                chunk_t = src[0, pl.ds(r0, LANES), :].astype(F32).T.astype(BF16)
                for kh in range(N_KV):
                    dst[i, kh, :HEAD_DIM] = chunk_t[kh * HEAD_DIM:(kh + 1) * HEAD_DIM]
                    dst[i, kh, HEAD_DIM:] = jnp.ones((ONES_ROWS, LANES), BF16)
            return carry
        lax.fori_loop(0, seq // LANES, prep, 0)
        for i in range(n_cmp_pad // LANES):
            vc_t_ref[i] = vc_ref[0, i * LANES:(i + 1) * LANES, :].astype(F32).T.astype(BF16)

    qp = _dot(q_ref[0], place_ref[...]).astype(BF16)
    q_unit = [jnp.concatenate([qp[:, hh * LANES:(hh + 1) * LANES]
                               for hh in range(u * HEADS_PER_UNIT, (u + 1) * HEADS_PER_UNIT)],
                              axis=0) for u in range(N_UNITS)]
    unit_kv = [u * HEADS_PER_UNIT // GQA for u in range(N_UNITS)]

    def q_pos(rows):
        return t0 + (lax.broadcasted_iota(jnp.int32, (rows, unit_w), 1) & (Q_BLOCK - 1))

    n_idx = lax.broadcasted_iota(jnp.int32, (n_cmp_pad, unit_w), 0)
    cmp_mask = (n_idx * CMP_STRIDE + (CMP_BLOCK - 1)) <= q_pos(n_cmp_pad)
    kc = kc_ref[0]
    vc_t = jnp.concatenate([vc_t_ref[i] for i in range(n_cmp_pad // LANES)], axis=1)
    o_c = []
    p_sum = [None] * N_KV
    scores_c = [_dot_nt(kc, q_unit[u]) for u in range(N_UNITS)]
    for u in range(N_UNITS):
        kh = unit_kv[u]
        s = jnp.where(cmp_mask, scores_c[u], NEG)
        m = jnp.max(s, axis=0, keepdims=True)
        p = jnp.where(cmp_mask, jnp.exp2(s - m), 0.0)
        l = jnp.sum(p, axis=0, keepdims=True)
        p = p * (1.0 / jnp.where(l == 0.0, 1.0, l))
        o_c.append(_dot(vc_t[kh * HEAD_DIM:(kh + 1) * HEAD_DIM], p.astype(BF16)))
        for hh in range(HEADS_PER_UNIT):
            ph = p[:, hh * Q_BLOCK:(hh + 1) * Q_BLOCK]
            p_sum[kh] = ph if p_sum[kh] is None else p_sum[kh] + ph

    ratio = SEL_BLOCK // CMP_STRIDE
    span = CMP_BLOCK // CMP_STRIDE
    j_idx = lax.broadcasted_iota(jnp.int32, (SUBLANES, Q_BLOCK), 0)
    cur = (t0 + lax.broadcasted_iota(jnp.int32, (SUBLANES, Q_BLOCK), 1)) // SEL_BLOCK
    first_own_block = t0 // SEL_BLOCK
    n_grp = S_BLOCKS // SUBLANES
    q_aug = [None] * N_UNITS
    for kh in range(N_KV):
        psum_ref[kh, :SUBLANES, :] = jnp.zeros((SUBLANES, Q_BLOCK), F32)
        psum_ref[kh, SUBLANES:, :] = p_sum[kh]
        imp_t = psum_ref[kh, pl.ds(SUBLANES - (span - 1), S_BLOCKS, stride=ratio), :]
        for k in range(2 - span, ratio):
            imp_t = imp_t + psum_ref[kh, pl.ds(SUBLANES + k, S_BLOCKS, stride=ratio), :]
        score = []
        for r in range(n_grp):
            j = j_idx + r * SUBLANES
            forced = (j == 0) | (j == cur) | (j == cur - 1)
            sc = jnp.where(j <= cur, imp_t[r * SUBLANES:(r + 1) * SUBLANES], -FORCE)
            score.append(jnp.where(forced, FORCE, sc))
        bias_rows = []
        for r in range(n_grp):
            cnt = jnp.zeros((SUBLANES, Q_BLOCK), jnp.int32)
            for jp in range(S_BLOCKS):
                rp = jp // SUBLANES
                other = score[rp][jp % SUBLANES:jp % SUBLANES + 1, :]
                ge = (other >= score[r]).astype(jnp.int32)
                gt = (other > score[r]).astype(jnp.int32)
                if rp < r:
                    ahead = ge
                elif rp > r:
                    ahead = gt
                else:
                    ahead = jnp.where(j_idx > (jp % SUBLANES), ge, gt)
                cnt = cnt + ahead
            past = (j_idx + r * SUBLANES) < first_own_block
            bias_rows.append(jnp.where(past, jnp.where(cnt < N_SELECT, 0.0, NEG), NEG))
        zero_rows = [jnp.zeros((LANES - S_BLOCKS, Q_BLOCK), F32)]
        bias_t = jnp.concatenate(
            zero_rows + bias_rows if _spare_lane0(kh) else bias_rows + zero_rows, axis=0)
        bias_q = jnp.concatenate([bias_t.T.astype(BF16)] * HEADS_PER_UNIT, axis=0)
        own = _own_lanes(lax.broadcasted_iota(jnp.int32, bias_q.shape, 1), kh)
        for u in range(N_UNITS):
            if unit_kv[u] == kh:
                q_aug[u] = jnp.where(own, q_unit[u], bias_q)

    def value_products(v_ref, chunk0, n_chunks, p_of_unit):
        v_aug = [jnp.concatenate([v_ref[chunk0 + j, kh] for j in range(n_chunks)], axis=1)
                 if n_chunks > 1 else v_ref[chunk0, kh] for kh in range(N_KV)]
        return [_dot(v_aug[unit_kv[u]], p_of_unit(u)) for u in range(N_UNITS)]

    w0 = pl.multiple_of(jnp.maximum(t0 - WINDOW, 0), Q_BLOCK)
    kw = kw_ref[0, pl.ds(w0, WIN_SPAN), :]
    diff = q_pos(WIN_SPAN) - (w0 + lax.broadcasted_iota(jnp.int32, (WIN_SPAN, unit_w), 0))
    win_mask = (diff >= 0) & (diff < WINDOW)
    scores_w = [_dot_nt(kw, q_unit[u]) for u in range(N_UNITS)]
    p_w = [_softmax_cols(scores_w[u], win_mask) for u in range(N_UNITS)]
    o_w = [_normalize(r) for r in
           value_products(vw_t_ref, w0 // LANES, WIN_SPAN // LANES, lambda u: p_w[u])]

    key_d = lax.broadcasted_iota(jnp.int32, (Q_BLOCK, unit_w), 0)
    causal_d = (t0 + key_d) <= q_pos(Q_BLOCK)
    kd = ks_ref[0, pl.ds(pl.multiple_of(t0, Q_BLOCK), Q_BLOCK), :]
    scores_d = [_dot_nt(kd, q_unit[u]) for u in range(N_UNITS)]
    m_d, p_d = [], []
    for u in range(N_UNITS):
        s = jnp.where(causal_d, scores_d[u], NEG)
        m_d.append(jnp.max(s, axis=0, keepdims=True))
        p_d.append(jnp.exp2(s - m_d[u]).astype(BF16))
    acc_d = value_products(vs_t_ref, c, 1, lambda u: p_d[u])

    chunks = SEL_TILE // LANES
    last_tile = seq // SEL_TILE - 1

    def score_tile(tile, s_ref):
        k0 = pl.multiple_of(tile * SEL_TILE, SEL_TILE)
        kt = [ks_aug_ref[kh, pl.ds(k0, SEL_TILE), :] for kh in range(N_KV)]
        for u in range(N_UNITS):
            s_ref[u] = _dot_nt(kt[unit_kv[u]], q_aug[u])

    def stage(tile, s_cur, s_next, p_cur, p_prev, states):
        pv_prev = value_products(vs_t_ref, jnp.maximum(tile - 1, 0) * chunks, chunks,
                                 lambda u: p_prev[u])
        score_tile(jnp.minimum(tile + 1, last_tile), s_next)
        out = []
        for u in range(N_UNITS):
            m, alpha_prev, acc = states[u]
            s = s_cur[u]
            m_new = jnp.maximum(m, jnp.max(s, axis=0, keepdims=True))
            p_cur[u] = jnp.exp2(s - m_new).astype(BF16)
            out.append((m_new, jnp.exp2(m - m_new), alpha_prev * acc + pv_prev[u]))
        return tuple(out)

    def pair_body(j, states):
        states = stage(2 * j, s_a_ref, s_b_ref, p_a_ref, p_b_ref, states)
        return stage(2 * j + 1, s_b_ref, s_a_ref, p_b_ref, p_a_ref, states)

    blocks_per_pair = 2 * SEL_TILE // Q_BLOCK
    n_pairs = (c + blocks_per_pair - 1) // blocks_per_pair
    score_tile(0, s_a_ref)
    p_b_ref[...] = jnp.zeros(p_b_ref.shape, BF16)
    init = tuple((m_d[u], jnp.ones((1, unit_w), F32), acc_d[u]) for u in range(N_UNITS))
    sel_states = lax.fori_loop(0, n_pairs, pair_body, init)
    pv_last = value_products(vs_t_ref, jnp.maximum(2 * n_pairs - 1, 0) * chunks, chunks,
                             lambda u: p_b_ref[u])
    o_s = [_normalize(alpha * acc + pv_last[u]) for u, (_, alpha, acc) in enumerate(sel_states)]

    gates_t = jax.nn.sigmoid(g_ref[0] + gb_ref[...]).T
    mixed = []
    for hh in range(H_B):
        u, sl = hh // HEADS_PER_UNIT, slice((hh % HEADS_PER_UNIT) * Q_BLOCK,
                                            (hh % HEADS_PER_UNIT + 1) * Q_BLOCK)
        mixed.append(gates_t[hh:hh + 1] * o_c[u][:, sl]
                     + gates_t[H_B + hh:H_B + hh + 1] * o_s[u][:, sl]
                     + gates_t[2 * H_B + hh:2 * H_B + hh + 1] * o_w[u][:, sl])
    bmix = jnp.concatenate(mixed, axis=0).T
    o_ref[0] = (bmix * _rms_scale(bmix) * onw_ref[...]).astype(BF16)


def _nsa(q3, kcv, kv3, g3, gate_b, onw, place):
    b, s, _ = q3.shape
    n_cmp_pad = kcv.shape[2]
    unit_w = HEADS_PER_UNIT * Q_BLOCK
    const2 = lambda bi, ci: (0, 0)
    kv_spec = lambda col: pl.BlockSpec((1, s, D_KV), lambda bi, ci: (bi, 0, col))
    return pl.pallas_call(
        _nsa_kernel,
        grid=(b, s // Q_BLOCK),
        in_specs=[
            pl.BlockSpec((1, Q_BLOCK, D_B), lambda bi, ci: (bi, ci, 0)),
            pl.BlockSpec((None, 1, n_cmp_pad, D_KV), lambda bi, ci: (bi, 0, 0, 0)),
            pl.BlockSpec((None, 1, n_cmp_pad, D_KV), lambda bi, ci: (bi, 1, 0, 0)),
            kv_spec(0), kv_spec(1), kv_spec(2), kv_spec(3),
            pl.BlockSpec((1, Q_BLOCK, LANES), lambda bi, ci: (bi, ci, 0)),
            pl.BlockSpec((1, LANES), const2),
            pl.BlockSpec((1, D_B), const2),
            pl.BlockSpec(place.shape, const2),
        ],
        out_specs=pl.BlockSpec((1, Q_BLOCK, D_B), lambda bi, ci: (bi, ci, 0)),
        out_shape=jax.ShapeDtypeStruct((b, s, D_B), BF16),
        scratch_shapes=[
            pltpu.VMEM((N_KV, s, LANES), BF16),
            pltpu.VMEM((s // LANES, N_KV, V_ROWS, LANES), BF16),
            pltpu.VMEM((s // LANES, N_KV, V_ROWS, LANES), BF16),
            pltpu.VMEM((n_cmp_pad // LANES, D_KV, LANES), BF16),
            pltpu.VMEM((N_KV, SUBLANES + n_cmp_pad, Q_BLOCK), F32),
            pltpu.VMEM((N_UNITS, SEL_TILE, unit_w), F32),
            pltpu.VMEM((N_UNITS, SEL_TILE, unit_w), F32),
            pltpu.VMEM((N_UNITS, SEL_TILE, unit_w), BF16),
            pltpu.VMEM((N_UNITS, SEL_TILE, unit_w), BF16),
        ],
        compiler_params=pltpu.CompilerParams(
            dimension_semantics=("arbitrary", "arbitrary"), vmem_limit_bytes=VMEM_LIMIT_BYTES),
        name="nsa",
    )(q3, kcv, kcv, kv3, kv3, kv3, kv3, g3, gate_b, onw, place)


def _mix_ffn_kernel(x_ref, a_ref, b_ref, nw_ref, fnw_ref, woa_hbm, wob_hbm, wg_hbm, wu_hbm,
                    wd_hbm, o_ref, woa_ref, wob_ref, wg_ref, wu_ref, wd_ref, sem, *, final_norm):
    @pl.when(pl.program_id(0) == 0)
    def _load_weights():
        pairs = ((woa_hbm, woa_ref), (wob_hbm, wob_ref), (wg_hbm, wg_ref), (wu_hbm, wu_ref),
                 (wd_hbm, wd_ref))
        copies = [pltpu.make_async_copy(src, dst, sem.at[i]) for i, (src, dst) in enumerate(pairs)]
        for cp in copies:
            cp.start()
        for cp in copies:
            cp.wait()

    x1 = x_ref[...] + _dot(a_ref[...], woa_ref[...]) + _dot(b_ref[...], wob_ref[...])
    h = (x1 * _rms_scale(x1) * nw_ref[...]).astype(BF16)
    gate = _dot(h, wg_ref[...])
    up = _dot(h, wu_ref[...])
    act = (gate * jax.nn.sigmoid(gate) * up).astype(BF16)
    acc = x1 + _dot(act, wd_ref[...])
    if final_norm:
        acc = acc * _rms_scale(acc) * fnw_ref[...]
    o_ref[...] = acc


def _mix_ffn(x2d, a2d, b2d, woa, wob, nw, wg, wu, wd, fnw, final_norm):
    t = x2d.shape[0]
    row = lambda i: (i, 0)
    const = lambda i: (0, 0)
    weights = (woa, wob, wg, wu, wd)
    in_hbm = pl.BlockSpec(memory_space=pl.ANY)
    return pl.pallas_call(
        functools.partial(_mix_ffn_kernel, final_norm=final_norm),
        grid=(t // TM_FFN,),
        in_specs=[
            pl.BlockSpec((TM_FFN, D_MODEL), row),
            pl.BlockSpec((TM_FFN, D_A), row),
            pl.BlockSpec((TM_FFN, D_B), row),
            pl.BlockSpec((1, D_MODEL), const),
            pl.BlockSpec((1, D_MODEL), const),
        ] + [in_hbm] * len(weights),
        out_specs=pl.BlockSpec((TM_FFN, D_MODEL), row),
        out_shape=jax.ShapeDtypeStruct((t, D_MODEL), F32),
        scratch_shapes=[pltpu.VMEM(wt.shape, wt.dtype) for wt in weights]
        + [pltpu.SemaphoreType.DMA((len(weights),))],
        compiler_params=pltpu.CompilerParams(
            dimension_semantics=("arbitrary",), vmem_limit_bytes=VMEM_LIMIT_BYTES),
        name="mix_ffn",
    )(x2d, a2d, b2d, nw, fnw, *weights)


def _placement_matrix():
    p = np.zeros((D_B, H_B * LANES), np.float32)
    for hh in range(H_B):
        for d in range(HEAD_DIM):
            p[hh * HEAD_DIM + d, hh * LANES + (hh // GQA) * HEAD_DIM + d] = 1.0
    return jnp.asarray(p, BF16)


def kernel(x, norm_mix_w, w_in, gmlp_norm_w, gmlp_ws, gmlp_bs, cmp_pos_k, cmp_pos_v, cmp_k_w1,
           cmp_k_w2, cmp_v_w1, cmp_v_w2, gate_b, out_norm_a_w, out_norm_b_w, w_o, norm_ffn_w,
           w_gate, w_up, w_down, final_norm_w):
    b, s, d = x.shape
    assert d == D_MODEL and s // SEL_BLOCK == S_BLOCKS and s % Q_BLOCK == 0
    assert (b * s) % TM_PROJ == 0 and (b * s) % TM_FFN == 0 and s % (GMLP_CHUNKS * CHUNK) == 0
    t = b * s
    place = _placement_matrix()
    o_uv, o_q = 2 * D_A, 2 * D_A + D_B
    o_c, o_kv = o_q + 2 * D_KV, o_q + 6 * D_KV

    x2d = x.reshape(t, d)
    for l in range(DEPTH):
        w = w_in[l]
        wg_pad = jnp.pad(w[:, o_kv:], ((0, 0), (0, LANES - N_GATES * H_B)))
        uv, q, kc_raw, kv, g_raw = _in_proj(
            x2d, norm_mix_w[l][None, :], w[:, :o_uv].astype(BF16), w[:, o_uv:o_q].astype(BF16),
            w[:, o_q:o_c].astype(BF16), w[:, o_c:o_kv].astype(BF16), wg_pad.astype(BF16))

        bst = jnp.pad(gmlp_bs[l].T, ((0, 0), (0, LANES - H_A)))
        a_n = _gmlp(uv.reshape(b, s, 2 * D_A), gmlp_norm_w[l][None, :], gmlp_ws[l], bst,
                    out_norm_a_w[l][None, :])

        kv3 = kv.reshape(b, s, 4 * D_KV)
        pos = jnp.tile(jnp.stack([cmp_pos_k[l], cmp_pos_v[l]]), (1, 1, N_KV))
        w1 = jnp.stack([cmp_k_w1[l], cmp_v_w1[l]]).reshape(2, CMP_BLOCK, HEAD_DIM, CMP_HIDDEN)
        w1 = _per_kv_head_blockdiag(w1).astype(BF16)
        w2 = _per_kv_head_blockdiag(jnp.stack([cmp_k_w2[l], cmp_v_w2[l]])).astype(BF16)
        kcv = _compress(kc_raw.reshape(b, s, 2 * D_KV), pos, w1, w2)

        gb_pad = jnp.pad(gate_b[l], (0, LANES - N_GATES * H_B))[None, :]
        b_n = _nsa(q.reshape(b, s, D_B), kcv, kv3, g_raw.reshape(b, s, LANES), gb_pad,
                   out_norm_b_w[l][None, :], place)

        x2d = _mix_ffn(
            x2d, a_n.reshape(t, D_A), b_n.reshape(t, D_B),
            w_o[l][:D_A].astype(BF16), w_o[l][D_A:].astype(BF16), norm_ffn_w[l][None, :],
            w_gate[l].astype(BF16), w_up[l].astype(BF16), w_down[l].astype(BF16),
            final_norm_w[None, :], final_norm=(l == DEPTH - 1))
    return x2d.reshape(b, s, d)
```

```python
import functools

import jax
import jax.numpy as jnp
import numpy as np
from jax import lax
from jax.experimental import pallas as pl
from jax.experimental.pallas import tpu as pltpu

D_MODEL = 1024
DEPTH = 2
D_A = 512
D_B = 512
HEAD_DIM = 64
H_A = D_A // HEAD_DIM
H_B = D_B // HEAD_DIM
N_KV = 2
GQA = H_B // N_KV
D_KV = N_KV * HEAD_DIM
CHUNK = 128
CMP_BLOCK = 32
CMP_STRIDE = 16
CMP_HIDDEN = 256
SEL_BLOCK = 64
N_SELECT = 16
WINDOW = 512
Q_BLOCK = 128
N_GATES = 3
D_FF = 2816
EPS = 1e-6
NEG = -1e30
FORCE = 1e4
Q_SCALE = HEAD_DIM ** -0.5 * float(np.log2(np.e))

LANES = 128
SUBLANES = 8
VMEM_LIMIT_BYTES = 56 * 1024 * 1024

TM_PROJ = 1024
TM_FFN = 512
GMLP_CHUNKS = 8
FF_CHUNK = 256
SEL_TILE = 256
SEL_TILES_PER_TRIP = 2
ONES_ROWS = 16
V_ROWS = HEAD_DIM + ONES_ROWS
HEADS_PER_UNIT = 2
N_UNITS = H_B // HEADS_PER_UNIT
WIN_SPAN = WINDOW + Q_BLOCK

F32 = jnp.float32
BF16 = jnp.bfloat16


def _gelu_tanh(x):
    c = np.float32(np.sqrt(2.0 / np.pi))
    return x * (0.5 * (1.0 + jnp.tanh(c * (x + 0.044715 * (x * x * x)))))


def _rms_scale(x):
    return lax.rsqrt(jnp.mean(x * x, axis=-1, keepdims=True) + EPS)


def _dot(a, b):
    return jnp.dot(a, b, preferred_element_type=F32)


def _dot_nt(a, b):
    return lax.dot_general(a, b, (((1,), (1,)), ((), ())), preferred_element_type=F32)


def _in_proj_kernel(x_ref, nw_ref, wuv_ref, wq_ref, wc_ref, wkv_ref, wg_ref,
                    uv_ref, q_ref, c_ref, kv_ref, g_ref):
    x = x_ref[...]
    h = (x * _rms_scale(x) * nw_ref[...]).astype(BF16)
    uv_ref[...] = _dot(h, wuv_ref[...])
    q_ref[...] = (_dot(h, wq_ref[...]) * Q_SCALE).astype(BF16)
    c_ref[...] = _dot(h, wc_ref[...])
    kv_ref[...] = _dot(h, wkv_ref[...]).astype(BF16)
    g_ref[...] = _dot(h, wg_ref[...])


def _in_proj(x2d, nw, wuv, wq, wc, wkv, wg):
    t = x2d.shape[0]
    const = lambda i: (0, 0)
    row = lambda i: (i, 0)
    return pl.pallas_call(
        _in_proj_kernel,
        grid=(t // TM_PROJ,),
        in_specs=[
            pl.BlockSpec((TM_PROJ, D_MODEL), row),
            pl.BlockSpec((1, D_MODEL), const),
            pl.BlockSpec((D_MODEL, 2 * D_A), const),
            pl.BlockSpec((D_MODEL, D_B), const),
            pl.BlockSpec((D_MODEL, 2 * D_KV), const),
            pl.BlockSpec((D_MODEL, 4 * D_KV), const),
            pl.BlockSpec((D_MODEL, LANES), const),
        ],
        out_specs=[
            pl.BlockSpec((TM_PROJ, 2 * D_A), row),
            pl.BlockSpec((TM_PROJ, D_B), row),
            pl.BlockSpec((TM_PROJ, 2 * D_KV), row),
            pl.BlockSpec((TM_PROJ, 4 * D_KV), row),
            pl.BlockSpec((TM_PROJ, LANES), row),
        ],
        out_shape=[
            jax.ShapeDtypeStruct((t, 2 * D_A), F32),
            jax.ShapeDtypeStruct((t, D_B), BF16),
            jax.ShapeDtypeStruct((t, 2 * D_KV), F32),
            jax.ShapeDtypeStruct((t, 4 * D_KV), BF16),
            jax.ShapeDtypeStruct((t, LANES), F32),
        ],
        compiler_params=pltpu.CompilerParams(
            dimension_semantics=("parallel",), vmem_limit_bytes=VMEM_LIMIT_BYTES),
        name="in_proj",
    )(x2d, nw, wuv, wq, wc, wkv, wg)


def _gmlp_kernel(u_ref, v_ref, nw_ref, ws_ref, bst_ref, onw_ref, o_ref):
    row = lax.broadcasted_iota(jnp.int32, (CHUNK, CHUNK), 0)
    col = lax.broadcasted_iota(jnp.int32, (CHUNK, CHUNK), 1)
    causal = col <= row
    low_half = col < HEAD_DIM
    bst = bst_ref[...]
    w = [jnp.where(causal, ws_ref[hh], 0.0).astype(BF16) for hh in range(H_A)]
    for ci in range(GMLP_CHUNKS):
        rows = slice(ci * CHUNK, (ci + 1) * CHUNK)
        u = _gelu_tanh(u_ref[0, rows, :])
        v = _gelu_tanh(v_ref[0, rows, :])
        vn = (v * _rms_scale(v) * nw_ref[...]).astype(BF16)
        tiles = []
        for i in range(D_A // LANES):
            vt = vn[:, i * LANES:(i + 1) * LANES]
            pair = [_dot(w[hh], vt) + bst[:, hh:hh + 1] for hh in (2 * i, 2 * i + 1)]
            tiles.append(jnp.where(low_half, pair[0], pair[1]))
        a = u * jnp.concatenate(tiles, axis=1)
        o_ref[0, rows, :] = (a * _rms_scale(a) * onw_ref[...]).astype(BF16)


def _gmlp(uv3, nw, ws, bst, onw):
    b, s, _ = uv3.shape
    rows = GMLP_CHUNKS * CHUNK
    const2 = lambda bi, ci: (0, 0)
    return pl.pallas_call(
        _gmlp_kernel,
        grid=(b, s // rows),
        in_specs=[
            pl.BlockSpec((1, rows, D_A), lambda bi, ci: (bi, ci, 0)),
            pl.BlockSpec((1, rows, D_A), lambda bi, ci: (bi, ci, 1)),
            pl.BlockSpec((1, D_A), const2),
            pl.BlockSpec((H_A, CHUNK, CHUNK), lambda bi, ci: (0, 0, 0)),
            pl.BlockSpec((CHUNK, LANES), const2),
            pl.BlockSpec((1, D_A), const2),
        ],
        out_specs=pl.BlockSpec((1, rows, D_A), lambda bi, ci: (bi, ci, 0)),
        out_shape=jax.ShapeDtypeStruct((b, s, D_A), BF16),
        compiler_params=pltpu.CompilerParams(
            dimension_semantics=("parallel", "parallel"), vmem_limit_bytes=VMEM_LIMIT_BYTES),
        name="gmlp",
    )(uv3, uv3, nw, ws, bst, onw)


def _compress_kernel(x_ref, pos_ref, w1_ref, w2_ref, o_ref, h1_ref, h2_ref):
    n_grp = o_ref.shape[2]
    for r in range(CMP_STRIDE):
        xr = x_ref[0, pl.ds(r, n_grp, stride=CMP_STRIDE), :]
        lo = _dot((xr + pos_ref[0, r:r + 1, :]).astype(BF16), w1_ref[0, r])
        hi = _dot((xr + pos_ref[0, CMP_STRIDE + r:CMP_STRIDE + r + 1, :]).astype(BF16),
                  w1_ref[0, CMP_STRIDE + r])
        if r == 0:
            h1_ref[...] = lo
            h2_ref[...] = hi
        else:
            h1_ref[...] += lo
            h2_ref[...] += hi
    h2_next = pltpu.roll(h2_ref[...], n_grp - 1, 0)
    hid = _gelu_tanh(h1_ref[...] + h2_next).astype(BF16)
    o_ref[0, 0] = _dot(hid, w2_ref[0]).astype(BF16)


def _compress(x3, pos, w1, w2):
    b, s, _ = x3.shape
    n_grp = s // CMP_STRIDE
    return pl.pallas_call(
        _compress_kernel,
        grid=(b, 2),
        in_specs=[
            pl.BlockSpec((1, s, D_KV), lambda bi, ti: (bi, 0, ti)),
            pl.BlockSpec((1,) + pos.shape[1:], lambda bi, ti: (ti, 0, 0)),
            pl.BlockSpec((1,) + w1.shape[1:], lambda bi, ti: (ti, 0, 0, 0)),
            pl.BlockSpec((1,) + w2.shape[1:], lambda bi, ti: (ti, 0, 0)),
        ],
        out_specs=pl.BlockSpec((1, 1, n_grp, D_KV), lambda bi, ti: (bi, ti, 0, 0)),
        out_shape=jax.ShapeDtypeStruct((b, 2, n_grp, D_KV), BF16),
        scratch_shapes=[pltpu.VMEM((n_grp, N_KV * CMP_HIDDEN), F32)] * 2,
        compiler_params=pltpu.CompilerParams(
            dimension_semantics=("parallel", "parallel"), vmem_limit_bytes=VMEM_LIMIT_BYTES),
        name="compress",
    )(x3, pos, w1, w2)


def _per_kv_head_blockdiag(w):
    k, n = w.shape[-2:]
    out = jnp.zeros(w.shape[:-2] + (N_KV * k, N_KV * n), w.dtype)
    for kh in range(N_KV):
        out = out.at[..., kh * k:(kh + 1) * k, kh * n:(kh + 1) * n].set(w)
    return out


S_BLOCKS = 64


def _softmax_cols(s, mask):
    s = jnp.where(mask, s, NEG)
    return jnp.exp2(s - jnp.max(s, axis=0, keepdims=True)).astype(BF16)


def _own_lanes(lane, kh):
    return (lane >= kh * HEAD_DIM) & (lane < (kh + 1) * HEAD_DIM)


def _spare_lane0(kh):
    return (1 - kh) * HEAD_DIM


def _normalize(r):
    return r[:HEAD_DIM] * (1.0 / r[HEAD_DIM:HEAD_DIM + 1])


def _nsa_kernel(q_ref, kc_ref, vc_ref, ks_ref, vs_ref, kw_ref, vw_ref, g_ref, gb_ref,
                onw_ref, place_ref, o_ref, ks_aug_ref, vs_t_ref, vw_t_ref, vc_t_ref, psum_ref,
                s_a_ref, s_b_ref, p_a_ref, p_b_ref):
    c = pl.program_id(1)
    t0 = c * Q_BLOCK
    n_cmp_pad = kc_ref.shape[1]
    seq = ks_ref.shape[1]
    unit_w = HEADS_PER_UNIT * Q_BLOCK

    @pl.when(c == 0)
    def _prepare_batch_row():
        lane = lax.broadcasted_iota(jnp.int32, (LANES, LANES), 1)
        blk_of_row = lax.broadcasted_iota(jnp.int32, (LANES, LANES), 0) // SEL_BLOCK

        def prep(i, carry):
            r0 = pl.multiple_of(i * LANES, LANES)
            keys = ks_ref[0, pl.ds(r0, LANES), :]
            key_blk = blk_of_row + i * (LANES // SEL_BLOCK)
            for kh in range(N_KV):
                one_hot = jnp.where(lane - _spare_lane0(kh) == key_blk, 1.0, 0.0).astype(BF16)
                ks_aug_ref[kh, pl.ds(r0, LANES), :] = jnp.where(_own_lanes(lane, kh), keys, one_hot)
            for src, dst in ((vs_ref, vs_t_ref), (vw_ref, vw_t_ref)):
                chunk_t = src[0, pl.ds(r0, LANES), :].astype(F32).T.astype(BF16)
                for kh in range(N_KV):
                    dst[i, kh, :HEAD_DIM] = chunk_t[kh * HEAD_DIM:(kh + 1) * HEAD_DIM]
                    dst[i, kh, HEAD_DIM:] = jnp.ones((ONES_ROWS, LANES), BF16)
            return carry
        lax.fori_loop(0, seq // LANES, prep, 0)
        for i in range(n_cmp_pad // LANES):
            vc_t_ref[i] = vc_ref[0, i * LANES:(i + 1) * LANES, :].astype(F32).T.astype(BF16)

    qp = _dot(q_ref[0], place_ref[...]).astype(BF16)
    q_unit = [jnp.concatenate([qp[:, hh * LANES:(hh + 1) * LANES]
                               for hh in range(u * HEADS_PER_UNIT, (u + 1) * HEADS_PER_UNIT)],
                              axis=0) for u in range(N_UNITS)]
    unit_kv = [u * HEADS_PER_UNIT // GQA for u in range(N_UNITS)]
    gates_t = jax.nn.sigmoid(g_ref[0] + gb_ref[...]).T

    def q_pos(rows):
        return t0 + (lax.broadcasted_iota(jnp.int32, (rows, unit_w), 1) & (Q_BLOCK - 1))

    def value_products(v_ref, chunk0, n_chunks, p_of_unit):
        v_aug = [jnp.concatenate([v_ref[chunk0 + j, kh] for j in range(n_chunks)], axis=1)
                 if n_chunks > 1 else v_ref[chunk0, kh] for kh in range(N_KV)]
        return [_dot(v_aug[unit_kv[u]], p_of_unit(u)) for u in range(N_UNITS)]

    w0 = pl.multiple_of(jnp.maximum(t0 - WINDOW, 0), Q_BLOCK)
    kc = kc_ref[0]
    kw = kw_ref[0, pl.ds(w0, WIN_SPAN), :]
    kd = ks_ref[0, pl.ds(pl.multiple_of(t0, Q_BLOCK), Q_BLOCK), :]
    scores_c = [_dot_nt(kc, q_unit[u]) for u in range(N_UNITS)]
    scores_w = [_dot_nt(kw, q_unit[u]) for u in range(N_UNITS)]
    scores_d = [_dot_nt(kd, q_unit[u]) for u in range(N_UNITS)]

    n_idx = lax.broadcasted_iota(jnp.int32, (n_cmp_pad, unit_w), 0)
    cmp_mask = (n_idx * CMP_STRIDE + (CMP_BLOCK - 1)) <= q_pos(n_cmp_pad)
    p_c = []
    p_sum = [None] * N_KV
    for u in range(N_UNITS):
        kh = unit_kv[u]
        s = jnp.where(cmp_mask, scores_c[u], NEG)
        m = jnp.max(s, axis=0, keepdims=True)
        p = jnp.where(cmp_mask, jnp.exp2(s - m), 0.0)
        l = jnp.sum(p, axis=0, keepdims=True)
        p = p * (1.0 / jnp.where(l == 0.0, 1.0, l))
        p_c.append(p.astype(BF16))
        for hh in range(HEADS_PER_UNIT):
            ph = p[:, hh * Q_BLOCK:(hh + 1) * Q_BLOCK]
            p_sum[kh] = ph if p_sum[kh] is None else p_sum[kh] + ph

    diff = q_pos(WIN_SPAN) - (w0 + lax.broadcasted_iota(jnp.int32, (WIN_SPAN, unit_w), 0))
    win_mask = (diff >= 0) & (diff < WINDOW)
    p_w = [_softmax_cols(scores_w[u], win_mask) for u in range(N_UNITS)]

    key_d = lax.broadcasted_iota(jnp.int32, (Q_BLOCK, unit_w), 0)
    causal_d = (t0 + key_d) <= q_pos(Q_BLOCK)
    m_d, p_d = [], []
    for u in range(N_UNITS):
        s = jnp.where(causal_d, scores_d[u], NEG)
        m_d.append(jnp.max(s, axis=0, keepdims=True))
        p_d.append(jnp.exp2(s - m_d[u]).astype(BF16))

    vc_t = jnp.concatenate([vc_t_ref[i] for i in range(n_cmp_pad // LANES)], axis=1)
    o_c = [_dot(vc_t[unit_kv[u] * HEAD_DIM:(unit_kv[u] + 1) * HEAD_DIM], p_c[u])
           for u in range(N_UNITS)]
    r_w = value_products(vw_t_ref, w0 // LANES, WIN_SPAN // LANES, lambda u: p_w[u])
    acc_d = value_products(vs_t_ref, c, 1, lambda u: p_d[u])

    ratio = SEL_BLOCK // CMP_STRIDE
    span = CMP_BLOCK // CMP_STRIDE
    j_idx = lax.broadcasted_iota(jnp.int32, (SUBLANES, Q_BLOCK), 0)
    cur = (t0 + lax.broadcasted_iota(jnp.int32, (SUBLANES, Q_BLOCK), 1)) // SEL_BLOCK
    first_own_block = t0 // SEL_BLOCK
    n_grp = S_BLOCKS // SUBLANES
    q_aug = [None] * N_UNITS
    for kh in range(N_KV):
        psum_ref[kh, :SUBLANES, :] = jnp.zeros((SUBLANES, Q_BLOCK), F32)
        psum_ref[kh, SUBLANES:, :] = p_sum[kh]
        imp_t = psum_ref[kh, pl.ds(SUBLANES - (span - 1), S_BLOCKS, stride=ratio), :]
        for k in range(2 - span, ratio):
            imp_t = imp_t + psum_ref[kh, pl.ds(SUBLANES + k, S_BLOCKS, stride=ratio), :]
        score = []
        for r in range(n_grp):
            j = j_idx + r * SUBLANES
            forced = (j == 0) | (j == cur) | (j == cur - 1)
            sc = jnp.where(j <= cur, imp_t[r * SUBLANES:(r + 1) * SUBLANES], -FORCE)
            score.append(jnp.where(forced, FORCE, sc))
        bias_rows = []
        for r in range(n_grp):
            cnt = jnp.zeros((SUBLANES, Q_BLOCK), jnp.int32)
            for jp in range(S_BLOCKS):
                rp = jp // SUBLANES
                other = score[rp][jp % SUBLANES:jp % SUBLANES + 1, :]
                ge = (other >= score[r]).astype(jnp.int32)
                gt = (other > score[r]).astype(jnp.int32)
                if rp < r:
                    ahead = ge
                elif rp > r:
                    ahead = gt
                else:
                    ahead = jnp.where(j_idx > (jp % SUBLANES), ge, gt)
                cnt = cnt + ahead
            past = (j_idx + r * SUBLANES) < first_own_block
            bias_rows.append(jnp.where(past, jnp.where(cnt < N_SELECT, 0.0, NEG), NEG))
        zero_rows = [jnp.zeros((LANES - S_BLOCKS, Q_BLOCK), F32)]
        bias_t = jnp.concatenate(
            zero_rows + bias_rows if _spare_lane0(kh) else bias_rows + zero_rows, axis=0)
        bias_q = jnp.concatenate([bias_t.T.astype(BF16)] * HEADS_PER_UNIT, axis=0)
        own = _own_lanes(lax.broadcasted_iota(jnp.int32, bias_q.shape, 1), kh)
        for u in range(N_UNITS):
            if unit_kv[u] == kh:
                q_aug[u] = jnp.where(own, q_unit[u], bias_q)

    o_w = [_normalize(r) for r in r_w]

    chunks = SEL_TILE // LANES
    last_tile = seq // SEL_TILE - 1

    def score_tile(tile, s_ref):
        k0 = pl.multiple_of(tile * SEL_TILE, SEL_TILE)
        kt = [ks_aug_ref[kh, pl.ds(k0, SEL_TILE), :] for kh in range(N_KV)]
        for u in range(N_UNITS):
            s_ref[u] = _dot_nt(kt[unit_kv[u]], q_aug[u])

    def stage(tile, s_cur, s_next, p_cur, p_prev, states):
        pv_prev = value_products(vs_t_ref, jnp.maximum(tile - 1, 0) * chunks, chunks,
                                 lambda u: p_prev[u])
        score_tile(jnp.minimum(tile + 1, last_tile), s_next)
        out = []
        for u in range(N_UNITS):
            m, alpha_prev, acc = states[u]
            s = s_cur[u]
            m_new = jnp.maximum(m, jnp.max(s, axis=0, keepdims=True))
            p_cur[u] = jnp.exp2(s - m_new).astype(BF16)
            out.append((m_new, jnp.exp2(m - m_new), alpha_prev * acc + pv_prev[u]))
        return tuple(out)

    def trip_body(j, states):
        for i in range(0, SEL_TILES_PER_TRIP, 2):
            tile = j * SEL_TILES_PER_TRIP + i
            states = stage(tile, s_a_ref, s_b_ref, p_a_ref, p_b_ref, states)
            states = stage(tile + 1, s_b_ref, s_a_ref, p_b_ref, p_a_ref, states)
        return states

    blocks_per_trip = SEL_TILES_PER_TRIP * SEL_TILE // Q_BLOCK
    n_trips = (c + blocks_per_trip - 1) // blocks_per_trip
    score_tile(0, s_a_ref)
    p_b_ref[...] = jnp.zeros(p_b_ref.shape, BF16)
    init = tuple((m_d[u], jnp.ones((1, unit_w), F32), acc_d[u]) for u in range(N_UNITS))
    sel_states = lax.fori_loop(0, n_trips, trip_body, init)
    pv_last = value_products(vs_t_ref,
                             jnp.maximum(SEL_TILES_PER_TRIP * n_trips - 1, 0) * chunks, chunks,
                             lambda u: p_b_ref[u])
    o_s = [_normalize(alpha * acc + pv_last[u]) for u, (_, alpha, acc) in enumerate(sel_states)]

    mixed = []
    for hh in range(H_B):
        u, sl = hh // HEADS_PER_UNIT, slice((hh % HEADS_PER_UNIT) * Q_BLOCK,
                                            (hh % HEADS_PER_UNIT + 1) * Q_BLOCK)
        mixed.append(gates_t[hh:hh + 1] * o_c[u][:, sl]
                     + gates_t[H_B + hh:H_B + hh + 1] * o_s[u][:, sl]
                     + gates_t[2 * H_B + hh:2 * H_B + hh + 1] * o_w[u][:, sl])
    bmix = jnp.concatenate(mixed, axis=0).T
    o_ref[0] = (bmix * _rms_scale(bmix) * onw_ref[...]).astype(BF16)


def _nsa(q3, kcv, kv3, g3, gate_b, onw, place):
    b, s, _ = q3.shape
    n_cmp_pad = kcv.shape[2]
    unit_w = HEADS_PER_UNIT * Q_BLOCK
    const2 = lambda bi, ci: (0, 0)
    kv_spec = lambda col: pl.BlockSpec((1, s, D_KV), lambda bi, ci: (bi, 0, col))
    return pl.pallas_call(
        _nsa_kernel,
        grid=(b, s // Q_BLOCK),
        in_specs=[
            pl.BlockSpec((1, Q_BLOCK, D_B), lambda bi, ci: (bi, ci, 0)),
            pl.BlockSpec((None, 1, n_cmp_pad, D_KV), lambda bi, ci: (bi, 0, 0, 0)),
            pl.BlockSpec((None, 1, n_cmp_pad, D_KV), lambda bi, ci: (bi, 1, 0, 0)),
            kv_spec(0), kv_spec(1), kv_spec(2), kv_spec(3),
            pl.BlockSpec((1, Q_BLOCK, LANES), lambda bi, ci: (bi, ci, 0)),
            pl.BlockSpec((1, LANES), const2),
            pl.BlockSpec((1, D_B), const2),
            pl.BlockSpec(place.shape, const2),
        ],
        out_specs=pl.BlockSpec((1, Q_BLOCK, D_B), lambda bi, ci: (bi, ci, 0)),
        out_shape=jax.ShapeDtypeStruct((b, s, D_B), BF16),
        scratch_shapes=[
            pltpu.VMEM((N_KV, s, LANES), BF16),
            pltpu.VMEM((s // LANES, N_KV, V_ROWS, LANES), BF16),
            pltpu.VMEM((s // LANES, N_KV, V_ROWS, LANES), BF16),
            pltpu.VMEM((n_cmp_pad // LANES, D_KV, LANES), BF16),
            pltpu.VMEM((N_KV, SUBLANES + n_cmp_pad, Q_BLOCK), F32),
            pltpu.VMEM((N_UNITS, SEL_TILE, unit_w), F32),
            pltpu.VMEM((N_UNITS, SEL_TILE, unit_w), F32),
            pltpu.VMEM((N_UNITS, SEL_TILE, unit_w), BF16),
            pltpu.VMEM((N_UNITS, SEL_TILE, unit_w), BF16),
        ],
        compiler_params=pltpu.CompilerParams(
            dimension_semantics=("arbitrary", "arbitrary"), vmem_limit_bytes=VMEM_LIMIT_BYTES),
        name="nsa",
    )(q3, kcv, kcv, kv3, kv3, kv3, kv3, g3, gate_b, onw, place)


def _mix_ffn_kernel(x_ref, a_ref, b_ref, nw_ref, fnw_ref, woa_hbm, wob_hbm, wg_hbm, wu_hbm,
                    wd_hbm, o_ref, woa_ref, wob_ref, wg_ref, wu_ref, wd_ref, sem, *, final_norm):
    @pl.when(pl.program_id(0) == 0)
    def _load_weights():
        pairs = ((woa_hbm, woa_ref), (wob_hbm, wob_ref), (wg_hbm, wg_ref), (wu_hbm, wu_ref),
                 (wd_hbm, wd_ref))
        copies = [pltpu.make_async_copy(src, dst, sem.at[i]) for i, (src, dst) in enumerate(pairs)]
        for cp in copies:
            cp.start()
        for cp in copies:
            cp.wait()

    x1 = x_ref[...] + _dot(a_ref[...], woa_ref[...]) + _dot(b_ref[...], wob_ref[...])
    h = (x1 * _rms_scale(x1) * nw_ref[...]).astype(BF16)
    gate = _dot(h, wg_ref[...])
    up = _dot(h, wu_ref[...])
    act = (gate * jax.nn.sigmoid(gate) * up).astype(BF16)
    acc = x1 + _dot(act, wd_ref[...])
    if final_norm:
        acc = acc * _rms_scale(acc) * fnw_ref[...]
    o_ref[...] = acc


def _mix_ffn(x2d, a2d, b2d, woa, wob, nw, wg, wu, wd, fnw, final_norm):
    t = x2d.shape[0]
    row = lambda i: (i, 0)
    const = lambda i: (0, 0)
    weights = (woa, wob, wg, wu, wd)
    in_hbm = pl.BlockSpec(memory_space=pl.ANY)
    return pl.pallas_call(
        functools.partial(_mix_ffn_kernel, final_norm=final_norm),
        grid=(t // TM_FFN,),
        in_specs=[
            pl.BlockSpec((TM_FFN, D_MODEL), row),
            pl.BlockSpec((TM_FFN, D_A), row),
            pl.BlockSpec((TM_FFN, D_B), row),
            pl.BlockSpec((1, D_MODEL), const),
            pl.BlockSpec((1, D_MODEL), const),
        ] + [in_hbm] * len(weights),
        out_specs=pl.BlockSpec((TM_FFN, D_MODEL), row),
        out_shape=jax.ShapeDtypeStruct((t, D_MODEL), F32),
        scratch_shapes=[pltpu.VMEM(wt.shape, wt.dtype) for wt in weights]
        + [pltpu.SemaphoreType.DMA((len(weights),))],
        compiler_params=pltpu.CompilerParams(
            dimension_semantics=("arbitrary",), vmem_limit_bytes=VMEM_LIMIT_BYTES),
        name="mix_ffn",
    )(x2d, a2d, b2d, nw, fnw, *weights)


def _placement_matrix():
    p = np.zeros((D_B, H_B * LANES), np.float32)
    for hh in range(H_B):
        for d in range(HEAD_DIM):
            p[hh * HEAD_DIM + d, hh * LANES + (hh // GQA) * HEAD_DIM + d] = 1.0
    return jnp.asarray(p, BF16)


def kernel(x, norm_mix_w, w_in, gmlp_norm_w, gmlp_ws, gmlp_bs, cmp_pos_k, cmp_pos_v, cmp_k_w1,
           cmp_k_w2, cmp_v_w1, cmp_v_w2, gate_b, out_norm_a_w, out_norm_b_w, w_o, norm_ffn_w,
           w_gate, w_up, w_down, final_norm_w):
    b, s, d = x.shape
    assert d == D_MODEL and s // SEL_BLOCK == S_BLOCKS and s % Q_BLOCK == 0
    assert (b * s) % TM_PROJ == 0 and (b * s) % TM_FFN == 0 and s % (GMLP_CHUNKS * CHUNK) == 0
    t = b * s
    place = _placement_matrix()
    o_uv, o_q = 2 * D_A, 2 * D_A + D_B
    o_c, o_kv = o_q + 2 * D_KV, o_q + 6 * D_KV

    x2d = x.reshape(t, d)
    for l in range(DEPTH):
        w = w_in[l]
        wg_pad = jnp.pad(w[:, o_kv:], ((0, 0), (0, LANES - N_GATES * H_B)))
        uv, q, kc_raw, kv, g_raw = _in_proj(
            x2d, norm_mix_w[l][None, :], w[:, :o_uv].astype(BF16), w[:, o_uv:o_q].astype(BF16),
            w[:, o_q:o_c].astype(BF16), w[:, o_c:o_kv].astype(BF16), wg_pad.astype(BF16))

        bst = jnp.pad(gmlp_bs[l].T, ((0, 0), (0, LANES - H_A)))
        a_n = _gmlp(uv.reshape(b, s, 2 * D_A), gmlp_norm_w[l][None, :], gmlp_ws[l], bst,
                    out_norm_a_w[l][None, :])

        kv3 = kv.reshape(b, s, 4 * D_KV)
        pos = jnp.tile(jnp.stack([cmp_pos_k[l], cmp_pos_v[l]]), (1, 1, N_KV))
        w1 = jnp.stack([cmp_k_w1[l], cmp_v_w1[l]]).reshape(2, CMP_BLOCK, HEAD_DIM, CMP_HIDDEN)
        w1 = _per_kv_head_blockdiag(w1).astype(BF16)
        w2 = _per_kv_head_blockdiag(jnp.stack([cmp_k_w2[l], cmp_v_w2[l]])).astype(BF16)
        kcv = _compress(kc_raw.reshape(b, s, 2 * D_KV), pos, w1, w2)

        gb_pad = jnp.pad(gate_b[l], (0, LANES - N_GATES * H_B))[None, :]
        b_n = _nsa(q.reshape(b, s, D_B), kcv, kv3, g_raw.reshape(b, s, LANES), gb_pad,
                   out_norm_b_w[l][None, :], place)

        x2d = _mix_ffn(
            x2d, a_n.reshape(t, D_A), b_n.reshape(t, D_B),
            w_o[l][:D_A].astype(BF16), w_o[l][D_A:].astype(BF16), norm_ffn_w[l][None, :],
            w_gate[l].astype(BF16), w_up[l].astype(BF16), w_down[l].astype(BF16),
            final_norm_w[None, :], final_norm=(l == DEPTH - 1))
    return x2d.reshape(b, s, d)
```

```python
import functools

import jax
import jax.numpy as jnp
import numpy as np
from jax import lax
from jax.experimental import pallas as pl
from jax.experimental.pallas import tpu as pltpu

D_MODEL = 1024
DEPTH = 2
D_A = 512
D_B = 512
HEAD_DIM = 64
H_A = D_A // HEAD_DIM
H_B = D_B // HEAD_DIM
N_KV = 2
GQA = H_B // N_KV
D_KV = N_KV * HEAD_DIM
CHUNK = 128
CMP_BLOCK = 32
CMP_STRIDE = 16
CMP_HIDDEN = 256
SEL_BLOCK = 64
N_SELECT = 16
WINDOW = 512
Q_BLOCK = 128
N_GATES = 3
D_FF = 2816
EPS = 1e-6
NEG = -1e30
FORCE = 1e4
Q_SCALE = HEAD_DIM ** -0.5 * float(np.log2(np.e))

LANES = 128
SUBLANES = 8
VMEM_LIMIT_BYTES = 56 * 1024 * 1024

TM_PROJ = 1024
TM_FFN = 512
GMLP_CHUNKS = 8
FF_CHUNK = 256
SEL_TILE = 256
ONES_ROWS = 16
V_ROWS = HEAD_DIM + ONES_ROWS
HEADS_PER_UNIT = 2
N_UNITS = H_B // HEADS_PER_UNIT
WIN_SPAN = WINDOW + Q_BLOCK

F32 = jnp.float32
BF16 = jnp.bfloat16


def _gelu_tanh(x):
    c = np.float32(np.sqrt(2.0 / np.pi))
    return x * (0.5 * (1.0 + jnp.tanh(c * (x + 0.044715 * (x * x * x)))))


def _rms_scale(x):
    return lax.rsqrt(jnp.mean(x * x, axis=-1, keepdims=True) + EPS)


def _dot(a, b):
    return jnp.dot(a, b, preferred_element_type=F32)


def _dot_nt(a, b):
    return lax.dot_general(a, b, (((1,), (1,)), ((), ())), preferred_element_type=F32)


def _in_proj_kernel(x_ref, nw_ref, wuv_ref, wq_ref, wc_ref, wkv_ref, wg_ref,
                    uv_ref, q_ref, c_ref, kv_ref, g_ref):
    x = x_ref[...]
    h = (x * _rms_scale(x) * nw_ref[...]).astype(BF16)
    uv_ref[...] = _dot(h, wuv_ref[...])
    q_ref[...] = (_dot(h, wq_ref[...]) * Q_SCALE).astype(BF16)
    c_ref[...] = _dot(h, wc_ref[...])
    kv_ref[...] = _dot(h, wkv_ref[...]).astype(BF16)
    g_ref[...] = _dot(h, wg_ref[...])


def _in_proj(x2d, nw, wuv, wq, wc, wkv, wg):
    t = x2d.shape[0]
    const = lambda i: (0, 0)
    row = lambda i: (i, 0)
    return pl.pallas_call(
        _in_proj_kernel,
        grid=(t // TM_PROJ,),
        in_specs=[
            pl.BlockSpec((TM_PROJ, D_MODEL), row),
            pl.BlockSpec((1, D_MODEL), const),
            pl.BlockSpec((D_MODEL, 2 * D_A), const),
            pl.BlockSpec((D_MODEL, D_B), const),
            pl.BlockSpec((D_MODEL, 2 * D_KV), const),
            pl.BlockSpec((D_MODEL, 4 * D_KV), const),
            pl.BlockSpec((D_MODEL, LANES), const),
        ],
        out_specs=[
            pl.BlockSpec((TM_PROJ, 2 * D_A), row),
            pl.BlockSpec((TM_PROJ, D_B), row),
            pl.BlockSpec((TM_PROJ, 2 * D_KV), row),
            pl.BlockSpec((TM_PROJ, 4 * D_KV), row),
            pl.BlockSpec((TM_PROJ, LANES), row),
        ],
        out_shape=[
            jax.ShapeDtypeStruct((t, 2 * D_A), F32),
            jax.ShapeDtypeStruct((t, D_B), BF16),
            jax.ShapeDtypeStruct((t, 2 * D_KV), F32),
            jax.ShapeDtypeStruct((t, 4 * D_KV), BF16),
            jax.ShapeDtypeStruct((t, LANES), F32),
        ],
        compiler_params=pltpu.CompilerParams(
            dimension_semantics=("parallel",), vmem_limit_bytes=VMEM_LIMIT_BYTES),
        name="in_proj",
    )(x2d, nw, wuv, wq, wc, wkv, wg)


def _gmlp_kernel(u_ref, v_ref, nw_ref, ws_ref, bst_ref, onw_ref, o_ref):
    row = lax.broadcasted_iota(jnp.int32, (CHUNK, CHUNK), 0)
    col = lax.broadcasted_iota(jnp.int32, (CHUNK, CHUNK), 1)
    causal = col <= row
    low_half = col < HEAD_DIM
    bst = bst_ref[...]
    w = [jnp.where(causal, ws_ref[hh], 0.0).astype(BF16) for hh in range(H_A)]
    for ci in range(GMLP_CHUNKS):
        rows = slice(ci * CHUNK, (ci + 1) * CHUNK)
        u = _gelu_tanh(u_ref[0, rows, :])
        v = _gelu_tanh(v_ref[0, rows, :])
        vn = (v * _rms_scale(v) * nw_ref[...]).astype(BF16)
        tiles = []
        for i in range(D_A // LANES):
            vt = vn[:, i * LANES:(i + 1) * LANES]
            pair = [_dot(w[hh], vt) + bst[:, hh:hh + 1] for hh in (2 * i, 2 * i + 1)]
            tiles.append(jnp.where(low_half, pair[0], pair[1]))
        a = u * jnp.concatenate(tiles, axis=1)
        o_ref[0, rows, :] = (a * _rms_scale(a) * onw_ref[...]).astype(BF16)


def _gmlp(uv3, nw, ws, bst, onw):
    b, s, _ = uv3.shape
    rows = GMLP_CHUNKS * CHUNK
    const2 = lambda bi, ci: (0, 0)
    return pl.pallas_call(
        _gmlp_kernel,
        grid=(b, s // rows),
        in_specs=[
            pl.BlockSpec((1, rows, D_A), lambda bi, ci: (bi, ci, 0)),
            pl.BlockSpec((1, rows, D_A), lambda bi, ci: (bi, ci, 1)),
            pl.BlockSpec((1, D_A), const2),
            pl.BlockSpec((H_A, CHUNK, CHUNK), lambda bi, ci: (0, 0, 0)),
            pl.BlockSpec((CHUNK, LANES), const2),
            pl.BlockSpec((1, D_A), const2),
        ],
        out_specs=pl.BlockSpec((1, rows, D_A), lambda bi, ci: (bi, ci, 0)),
        out_shape=jax.ShapeDtypeStruct((b, s, D_A), BF16),
        compiler_params=pltpu.CompilerParams(
            dimension_semantics=("parallel", "parallel"), vmem_limit_bytes=VMEM_LIMIT_BYTES),
        name="gmlp",
    )(uv3, uv3, nw, ws, bst, onw)


def _compress_kernel(x_ref, pos_ref, w1_ref, w2_ref, o_ref, h1_ref, h2_ref):
    n_grp = o_ref.shape[2]
    for r in range(CMP_STRIDE):
        xr = x_ref[0, pl.ds(r, n_grp, stride=CMP_STRIDE), :]
        lo = _dot((xr + pos_ref[0, r:r + 1, :]).astype(BF16), w1_ref[0, r])
        hi = _dot((xr + pos_ref[0, CMP_STRIDE + r:CMP_STRIDE + r + 1, :]).astype(BF16),
                  w1_ref[0, CMP_STRIDE + r])
        if r == 0:
            h1_ref[...] = lo
            h2_ref[...] = hi
        else:
            h1_ref[...] += lo
            h2_ref[...] += hi
    h2_next = pltpu.roll(h2_ref[...], n_grp - 1, 0)
    hid = _gelu_tanh(h1_ref[...] + h2_next).astype(BF16)
    o_ref[0, 0] = _dot(hid, w2_ref[0]).astype(BF16)


def _compress(x3, pos, w1, w2):
    b, s, _ = x3.shape
    n_grp = s // CMP_STRIDE
    return pl.pallas_call(
        _compress_kernel,
        grid=(b, 2),
        in_specs=[
            pl.BlockSpec((1, s, D_KV), lambda bi, ti: (bi, 0, ti)),
            pl.BlockSpec((1,) + pos.shape[1:], lambda bi, ti: (ti, 0, 0)),
            pl.BlockSpec((1,) + w1.shape[1:], lambda bi, ti: (ti, 0, 0, 0)),
            pl.BlockSpec((1,) + w2.shape[1:], lambda bi, ti: (ti, 0, 0)),
        ],
        out_specs=pl.BlockSpec((1, 1, n_grp, D_KV), lambda bi, ti: (bi, ti, 0, 0)),
        out_shape=jax.ShapeDtypeStruct((b, 2, n_grp, D_KV), BF16),
        scratch_shapes=[pltpu.VMEM((n_grp, N_KV * CMP_HIDDEN), F32)] * 2,
        compiler_params=pltpu.CompilerParams(
            dimension_semantics=("parallel", "parallel"), vmem_limit_bytes=VMEM_LIMIT_BYTES),
        name="compress",
    )(x3, pos, w1, w2)


def _per_kv_head_blockdiag(w):
    k, n = w.shape[-2:]
    out = jnp.zeros(w.shape[:-2] + (N_KV * k, N_KV * n), w.dtype)
    for kh in range(N_KV):
        out = out.at[..., kh * k:(kh + 1) * k, kh * n:(kh + 1) * n].set(w)
    return out


S_BLOCKS = 64


def _softmax_cols(s, mask):
    s = jnp.where(mask, s, NEG)
    return jnp.exp2(s - jnp.max(s, axis=0, keepdims=True)).astype(BF16)


def _own_lanes(lane, kh):
    return (lane >= kh * HEAD_DIM) & (lane < (kh + 1) * HEAD_DIM)


def _spare_lane0(kh):
    return (1 - kh) * HEAD_DIM


def _normalize(r):
    return r[:HEAD_DIM] * (1.0 / r[HEAD_DIM:HEAD_DIM + 1])


def _nsa_kernel(q_ref, kc_ref, vc_ref, ks_ref, vs_ref, kw_ref, vw_ref, g_ref, gb_ref,
                onw_ref, place_ref, o_ref, ks_aug_ref, vs_t_ref, vw_t_ref, vc_t_ref, psum_ref,
                s_a_ref, s_b_ref, p_a_ref, p_b_ref):
    c = pl.program_id(1)
    t0 = c * Q_BLOCK
    n_cmp_pad = kc_ref.shape[1]
    seq = ks_ref.shape[1]
    unit_w = HEADS_PER_UNIT * Q_BLOCK

    @pl.when(c == 0)
    def _prepare_batch_row():
        lane = lax.broadcasted_iota(jnp.int32, (LANES, LANES), 1)
        blk_of_row = lax.broadcasted_iota(jnp.int32, (LANES, LANES), 0) // SEL_BLOCK

        def prep(i, carry):
            r0 = pl.multiple_of(i * LANES, LANES)
            keys = ks_ref[0, pl.ds(r0, LANES), :]
            key_blk = blk_of_row + i * (LANES // SEL_BLOCK)
            for kh in range(N_KV):
                one_hot = jnp.where(lane - _spare_lane0(kh) == key_blk, 1.0, 0.0).astype(BF16)
                ks_aug_ref[kh, pl.ds(r0, LANES), :] = jnp.where(_own_lanes(lane, kh), keys, one_hot)
            for src, dst in ((vs_ref, vs_t_ref), (vw_ref, vw_t_ref)):
                chunk_t = src[0, pl.ds(r0, LANES), :].astype(F32).T.astype(BF16)
                for kh in range(N_KV):
                    dst[i, kh, :HEAD_DIM] = chunk_t[kh * HEAD_DIM:(kh + 1) * HEAD_DIM]
                    dst[i, kh, HEAD_DIM:] = jnp.ones((ONES_ROWS, LANES), BF16)
            return carry
        lax.fori_loop(0, seq // LANES, prep, 0)
        for i in range(n_cmp_pad // LANES):
            vc_t_ref[i] = vc_ref[0, i * LANES:(i + 1) * LANES, :].astype(F32).T.astype(BF16)

    qp = _dot(q_ref[0], place_ref[...]).astype(BF16)
    q_unit = [jnp.concatenate([qp[:, hh * LANES:(hh + 1) * LANES]
                               for hh in range(u * HEADS_PER_UNIT, (u + 1) * HEADS_PER_UNIT)],
                              axis=0) for u in range(N_UNITS)]
    unit_kv = [u * HEADS_PER_UNIT // GQA for u in range(N_UNITS)]
    gates_t = jax.nn.sigmoid(g_ref[0] + gb_ref[...]).T

    def q_pos(rows):
        return t0 + (lax.broadcasted_iota(jnp.int32, (rows, unit_w), 1) & (Q_BLOCK - 1))

    def value_products(v_ref, chunk0, n_chunks, p_of_unit):
        v_aug = [jnp.concatenate([v_ref[chunk0 + j, kh] for j in range(n_chunks)], axis=1)
                 if n_chunks > 1 else v_ref[chunk0, kh] for kh in range(N_KV)]
        return [_dot(v_aug[unit_kv[u]], p_of_unit(u)) for u in range(N_UNITS)]

    w0 = pl.multiple_of(jnp.maximum(t0 - WINDOW, 0), Q_BLOCK)
    kc = kc_ref[0]
    kw = kw_ref[0, pl.ds(w0, WIN_SPAN), :]
    kd = ks_ref[0, pl.ds(pl.multiple_of(t0, Q_BLOCK), Q_BLOCK), :]
    scores_c = [_dot_nt(kc, q_unit[u]) for u in range(N_UNITS)]
    scores_w = [_dot_nt(kw, q_unit[u]) for u in range(N_UNITS)]
    scores_d = [_dot_nt(kd, q_unit[u]) for u in range(N_UNITS)]

    n_idx = lax.broadcasted_iota(jnp.int32, (n_cmp_pad, unit_w), 0)
    cmp_mask = (n_idx * CMP_STRIDE + (CMP_BLOCK - 1)) <= q_pos(n_cmp_pad)
    sees_a_block = q_pos(1) >= CMP_BLOCK - 1
    p_c = []
    p_sum = [None] * N_KV
    for u in range(N_UNITS):
        kh = unit_kv[u]
        s = jnp.where(cmp_mask, scores_c[u], NEG)
        m = jnp.max(s, axis=0, keepdims=True)
        p = jnp.exp2(s - m)
        l = jnp.sum(p, axis=0, keepdims=True)
        p = p * jnp.where(sees_a_block, 1.0 / l, 0.0)
        p_c.append(p.astype(BF16))
        for hh in range(HEADS_PER_UNIT):
            ph = p[:, hh * Q_BLOCK:(hh + 1) * Q_BLOCK]
            p_sum[kh] = ph if p_sum[kh] is None else p_sum[kh] + ph

    diff = q_pos(WIN_SPAN) - (w0 + lax.broadcasted_iota(jnp.int32, (WIN_SPAN, unit_w), 0))
    win_mask = (diff >= 0) & (diff < WINDOW)
    p_w = [_softmax_cols(scores_w[u], win_mask) for u in range(N_UNITS)]

    key_d = lax.broadcasted_iota(jnp.int32, (Q_BLOCK, unit_w), 0)
    causal_d = (t0 + key_d) <= q_pos(Q_BLOCK)
    m_d, p_d = [], []
    for u in range(N_UNITS):
        s = jnp.where(causal_d, scores_d[u], NEG)
        m_d.append(jnp.max(s, axis=0, keepdims=True))
        p_d.append(jnp.exp2(s - m_d[u]).astype(BF16))

    vc_t = jnp.concatenate([vc_t_ref[i] for i in range(n_cmp_pad // LANES)], axis=1)
    o_c = [_dot(vc_t[unit_kv[u] * HEAD_DIM:(unit_kv[u] + 1) * HEAD_DIM], p_c[u])
           for u in range(N_UNITS)]
    r_w = value_products(vw_t_ref, w0 // LANES, WIN_SPAN // LANES, lambda u: p_w[u])
    acc_d = value_products(vs_t_ref, c, 1, lambda u: p_d[u])

    ratio = SEL_BLOCK // CMP_STRIDE
    span = CMP_BLOCK // CMP_STRIDE
    j_idx = lax.broadcasted_iota(jnp.int32, (SUBLANES, Q_BLOCK), 0)
    cur = (t0 + lax.broadcasted_iota(jnp.int32, (SUBLANES, Q_BLOCK), 1)) // SEL_BLOCK
    first_own_block = t0 // SEL_BLOCK
    n_grp = S_BLOCKS // SUBLANES
    q_aug = [None] * N_UNITS
    for kh in range(N_KV):
        psum_ref[kh, :SUBLANES, :] = jnp.zeros((SUBLANES, Q_BLOCK), F32)
        psum_ref[kh, SUBLANES:, :] = p_sum[kh]
        imp_t = psum_ref[kh, pl.ds(SUBLANES - (span - 1), S_BLOCKS, stride=ratio), :]
        for k in range(2 - span, ratio):
            imp_t = imp_t + psum_ref[kh, pl.ds(SUBLANES + k, S_BLOCKS, stride=ratio), :]
        score = []
        for r in range(n_grp):
            j = j_idx + r * SUBLANES
            forced = (j == 0) | (j == cur) | (j == cur - 1)
            sc = jnp.where(j <= cur, imp_t[r * SUBLANES:(r + 1) * SUBLANES], -FORCE)
            score.append(jnp.where(forced, FORCE, sc))
        bias_rows = []
        for r in range(n_grp):
            cnt = jnp.zeros((SUBLANES, Q_BLOCK), jnp.int32)
            for jp in range(S_BLOCKS):
                rp = jp // SUBLANES
                other = score[rp][jp % SUBLANES:jp % SUBLANES + 1, :]
                ge = (other >= score[r]).astype(jnp.int32)
                gt = (other > score[r]).astype(jnp.int32)
                if rp < r:
                    ahead = ge
                elif rp > r:
                    ahead = gt
                else:
                    ahead = jnp.where(j_idx > (jp % SUBLANES), ge, gt)
                cnt = cnt + ahead
            past = (j_idx + r * SUBLANES) < first_own_block
            bias_rows.append(jnp.where(past, jnp.where(cnt < N_SELECT, 0.0, NEG), NEG))
        zero_rows = [jnp.zeros((LANES - S_BLOCKS, Q_BLOCK), F32)]
        bias_t = jnp.concatenate(
            zero_rows + bias_rows if _spare_lane0(kh) else bias_rows + zero_rows, axis=0)
        bias_q = jnp.concatenate([bias_t.T.astype(BF16)] * HEADS_PER_UNIT, axis=0)
        own = _own_lanes(lax.broadcasted_iota(jnp.int32, bias_q.shape, 1), kh)
        for u in range(N_UNITS):
            if unit_kv[u] == kh:
                q_aug[u] = jnp.where(own, q_unit[u], bias_q)

    o_w = [_normalize(r) for r in r_w]

    chunks = SEL_TILE // LANES
    last_tile = seq // SEL_TILE - 1

    def score_tile(tile, s_ref):
        k0 = pl.multiple_of(tile * SEL_TILE, SEL_TILE)
        kt = [ks_aug_ref[kh, pl.ds(k0, SEL_TILE), :] for kh in range(N_KV)]
        for u in range(N_UNITS):
            s_ref[u] = _dot_nt(kt[unit_kv[u]], q_aug[u])

    def stage(tile, s_cur, s_next, p_cur, p_prev, states, prefetch=True):
        pv_prev = value_products(vs_t_ref, jnp.maximum(tile - 1, 0) * chunks, chunks,
                                 lambda u: p_prev[u])
        if prefetch:
            score_tile(jnp.minimum(tile + 1, last_tile), s_next)
        out = []
        for u in range(N_UNITS):
            m, alpha_prev, acc = states[u]
            s = s_cur[u]
            m_new = jnp.maximum(m, jnp.max(s, axis=0, keepdims=True))
            p_cur[u] = jnp.exp2(s - m_new).astype(BF16)
            out.append((m_new, jnp.exp2(m - m_new), alpha_prev * acc + pv_prev[u]))
        return tuple(out)

    def trip(first_tile, states, prefetch_after=True):
        states = stage(first_tile, s_a_ref, s_b_ref, p_a_ref, p_b_ref, states)
        return stage(first_tile + 1, s_b_ref, s_a_ref, p_b_ref, p_a_ref, states, prefetch_after)

    blocks_per_trip = 2 * SEL_TILE // Q_BLOCK
    loop_trips = jnp.maximum((c + blocks_per_trip - 1) // blocks_per_trip - 1, 0)
    score_tile(0, s_a_ref)
    p_b_ref[...] = jnp.zeros(p_b_ref.shape, BF16)
    init = tuple((m_d[u], jnp.ones((1, unit_w), F32), acc_d[u]) for u in range(N_UNITS))
    sel_states = lax.fori_loop(0, loop_trips, lambda j, st: trip(2 * j, st), init)
    sel_states = trip(2 * loop_trips, sel_states, prefetch_after=False)
    pv_last = value_products(vs_t_ref, (2 * loop_trips + 1) * chunks, chunks,
                             lambda u: p_b_ref[u])
    o_s = [_normalize(alpha * acc + pv_last[u]) for u, (_, alpha, acc) in enumerate(sel_states)]

    mixed = []
    for hh in range(H_B):
        u, sl = hh // HEADS_PER_UNIT, slice((hh % HEADS_PER_UNIT) * Q_BLOCK,
                                            (hh % HEADS_PER_UNIT + 1) * Q_BLOCK)
        mixed.append(gates_t[hh:hh + 1] * o_c[u][:, sl]
                     + gates_t[H_B + hh:H_B + hh + 1] * o_s[u][:, sl]
                     + gates_t[2 * H_B + hh:2 * H_B + hh + 1] * o_w[u][:, sl])
    bmix = jnp.concatenate(mixed, axis=0).T
    o_ref[0] = (bmix * _rms_scale(bmix) * onw_ref[...]).astype(BF16)


def _nsa(q3, kcv, kv3, g3, gate_b, onw, place):
    b, s, _ = q3.shape
    n_cmp_pad = kcv.shape[2]
    unit_w = HEADS_PER_UNIT * Q_BLOCK
    const2 = lambda bi, ci: (0, 0)
    kv_spec = lambda col: pl.BlockSpec((1, s, D_KV), lambda bi, ci: (bi, 0, col))
    return pl.pallas_call(
        _nsa_kernel,
        grid=(b, s // Q_BLOCK),
        in_specs=[
            pl.BlockSpec((1, Q_BLOCK, D_B), lambda bi, ci: (bi, ci, 0)),
            pl.BlockSpec((None, 1, n_cmp_pad, D_KV), lambda bi, ci: (bi, 0, 0, 0)),
            pl.BlockSpec((None, 1, n_cmp_pad, D_KV), lambda bi, ci: (bi, 1, 0, 0)),
            kv_spec(0), kv_spec(1), kv_spec(2), kv_spec(3),
            pl.BlockSpec((1, Q_BLOCK, LANES), lambda bi, ci: (bi, ci, 0)),
            pl.BlockSpec((1, LANES), const2),
            pl.BlockSpec((1, D_B), const2),
            pl.BlockSpec(place.shape, const2),
        ],
        out_specs=pl.BlockSpec((1, Q_BLOCK, D_B), lambda bi, ci: (bi, ci, 0)),
        out_shape=jax.ShapeDtypeStruct((b, s, D_B), BF16),
        scratch_shapes=[
            pltpu.VMEM((N_KV, s, LANES), BF16),
            pltpu.VMEM((s // LANES, N_KV, V_ROWS, LANES), BF16),
            pltpu.VMEM((s // LANES, N_KV, V_ROWS, LANES), BF16),
            pltpu.VMEM((n_cmp_pad // LANES, D_KV, LANES), BF16),
            pltpu.VMEM((N_KV, SUBLANES + n_cmp_pad, Q_BLOCK), F32),
            pltpu.VMEM((N_UNITS, SEL_TILE, unit_w), F32),
            pltpu.VMEM((N_UNITS, SEL_TILE, unit_w), F32),
            pltpu.VMEM((N_UNITS, SEL_TILE, unit_w), BF16),
            pltpu.VMEM((N_UNITS, SEL_TILE, unit_w), BF16),
        ],
        compiler_params=pltpu.CompilerParams(
            dimension_semantics=("arbitrary", "arbitrary"), vmem_limit_bytes=VMEM_LIMIT_BYTES),
        name="nsa",
    )(q3, kcv, kcv, kv3, kv3, kv3, kv3, g3, gate_b, onw, place)


def _mix_ffn_kernel(x_ref, a_ref, b_ref, nw_ref, fnw_ref, woa_hbm, wob_hbm, wg_hbm, wu_hbm,
                    wd_hbm, o_ref, woa_ref, wob_ref, wg_ref, wu_ref, wd_ref, sem, *, final_norm):
    @pl.when(pl.program_id(0) == 0)
    def _load_weights():
        pairs = ((woa_hbm, woa_ref), (wob_hbm, wob_ref), (wg_hbm, wg_ref), (wu_hbm, wu_ref),
                 (wd_hbm, wd_ref))
        copies = [pltpu.make_async_copy(src, dst, sem.at[i]) for i, (src, dst) in enumerate(pairs)]
        for cp in copies:
            cp.start()
        for cp in copies:
            cp.wait()

    x1 = x_ref[...] + _dot(a_ref[...], woa_ref[...]) + _dot(b_ref[...], wob_ref[...])
    h = (x1 * _rms_scale(x1) * nw_ref[...]).astype(BF16)
    gate = _dot(h, wg_ref[...])
    up = _dot(h, wu_ref[...])
    act = (gate * jax.nn.sigmoid(gate) * up).astype(BF16)
    acc = x1 + _dot(act, wd_ref[...])
    if final_norm:
        acc = acc * _rms_scale(acc) * fnw_ref[...]
    o_ref[...] = acc


def _mix_ffn(x2d, a2d, b2d, woa, wob, nw, wg, wu, wd, fnw, final_norm):
    t = x2d.shape[0]
    row = lambda i: (i, 0)
    const = lambda i: (0, 0)
    weights = (woa, wob, wg, wu, wd)
    in_hbm = pl.BlockSpec(memory_space=pl.ANY)
    return pl.pallas_call(
        functools.partial(_mix_ffn_kernel, final_norm=final_norm),
        grid=(t // TM_FFN,),
        in_specs=[
            pl.BlockSpec((TM_FFN, D_MODEL), row),
            pl.BlockSpec((TM_FFN, D_A), row),
            pl.BlockSpec((TM_FFN, D_B), row),
            pl.BlockSpec((1, D_MODEL), const),
            pl.BlockSpec((1, D_MODEL), const),
        ] + [in_hbm] * len(weights),
        out_specs=pl.BlockSpec((TM_FFN, D_MODEL), row),
        out_shape=jax.ShapeDtypeStruct((t, D_MODEL), F32),
        scratch_shapes=[pltpu.VMEM(wt.shape, wt.dtype) for wt in weights]
        + [pltpu.SemaphoreType.DMA((len(weights),))],
        compiler_params=pltpu.CompilerParams(
            dimension_semantics=("arbitrary",), vmem_limit_bytes=VMEM_LIMIT_BYTES),
        name="mix_ffn",
    )(x2d, a2d, b2d, nw, fnw, *weights)


def _placement_matrix():
    p = np.zeros((D_B, H_B * LANES), np.float32)
    for hh in range(H_B):
        for d in range(HEAD_DIM):
            p[hh * HEAD_DIM + d, hh * LANES + (hh // GQA) * HEAD_DIM + d] = 1.0
    return jnp.asarray(p, BF16)


def kernel(x, norm_mix_w, w_in, gmlp_norm_w, gmlp_ws, gmlp_bs, cmp_pos_k, cmp_pos_v, cmp_k_w1,
           cmp_k_w2, cmp_v_w1, cmp_v_w2, gate_b, out_norm_a_w, out_norm_b_w, w_o, norm_ffn_w,
           w_gate, w_up, w_down, final_norm_w):
    b, s, d = x.shape
    assert d == D_MODEL and s // SEL_BLOCK == S_BLOCKS and s % Q_BLOCK == 0
    assert (b * s) % TM_PROJ == 0 and (b * s) % TM_FFN == 0 and s % (GMLP_CHUNKS * CHUNK) == 0
    t = b * s
    place = _placement_matrix()
    o_uv, o_q = 2 * D_A, 2 * D_A + D_B
    o_c, o_kv = o_q + 2 * D_KV, o_q + 6 * D_KV

    x2d = x.reshape(t, d)
    for l in range(DEPTH):
        w = w_in[l]
        wg_pad = jnp.pad(w[:, o_kv:], ((0, 0), (0, LANES - N_GATES * H_B)))
        uv, q, kc_raw, kv, g_raw = _in_proj(
            x2d, norm_mix_w[l][None, :], w[:, :o_uv].astype(BF16), w[:, o_uv:o_q].astype(BF16),
            w[:, o_q:o_c].astype(BF16), w[:, o_c:o_kv].astype(BF16), wg_pad.astype(BF16))

        bst = jnp.pad(gmlp_bs[l].T, ((0, 0), (0, LANES - H_A)))
        a_n = _gmlp(uv.reshape(b, s, 2 * D_A), gmlp_norm_w[l][None, :], gmlp_ws[l], bst,
                    out_norm_a_w[l][None, :])

        kv3 = kv.reshape(b, s, 4 * D_KV)
        pos = jnp.tile(jnp.stack([cmp_pos_k[l], cmp_pos_v[l]]), (1, 1, N_KV))
        w1 = jnp.stack([cmp_k_w1[l], cmp_v_w1[l]]).reshape(2, CMP_BLOCK, HEAD_DIM, CMP_HIDDEN)
        w1 = _per_kv_head_blockdiag(w1).astype(BF16)
        w2 = _per_kv_head_blockdiag(jnp.stack([cmp_k_w2[l], cmp_v_w2[l]])).astype(BF16)
        kcv = _compress(kc_raw.reshape(b, s, 2 * D_KV), pos, w1, w2)

        gb_pad = jnp.pad(gate_b[l], (0, LANES - N_GATES * H_B))[None, :]
        b_n = _nsa(q.reshape(b, s, D_B), kcv, kv3, g_raw.reshape(b, s, LANES), gb_pad,
                   out_norm_b_w[l][None, :], place)

        x2d = _mix_ffn(
            x2d, a_n.reshape(t, D_A), b_n.reshape(t, D_B),
            w_o[l][:D_A].astype(BF16), w_o[l][D_A:].astype(BF16), norm_ffn_w[l][None, :],
            w_gate[l].astype(BF16), w_up[l].astype(BF16), w_down[l].astype(BF16),
            final_norm_w[None, :], final_norm=(l == DEPTH - 1))
    return x2d.reshape(b, s, d)
```

```python
import functools

import jax
import jax.numpy as jnp
import numpy as np
from jax import lax
from jax.experimental import pallas as pl
from jax.experimental.pallas import tpu as pltpu

D_MODEL = 1024
DEPTH = 2
D_A = 512
D_B = 512
HEAD_DIM = 64
H_A = D_A // HEAD_DIM
H_B = D_B // HEAD_DIM
N_KV = 2
GQA = H_B // N_KV
D_KV = N_KV * HEAD_DIM
CHUNK = 128
CMP_BLOCK = 32
CMP_STRIDE = 16
CMP_HIDDEN = 256
SEL_BLOCK = 64
N_SELECT = 16
WINDOW = 512
Q_BLOCK = 128
N_GATES = 3
D_FF = 2816
EPS = 1e-6
NEG = -1e30
FORCE = 1e4
Q_SCALE = HEAD_DIM ** -0.5 * float(np.log2(np.e))

LANES = 128
SUBLANES = 8
VMEM_LIMIT_BYTES = 56 * 1024 * 1024

TM_PROJ = 1024
TM_FFN = 512
GMLP_CHUNKS = 8
FF_CHUNK = 256
SEL_TILE = 256
ONES_ROWS = 16
V_ROWS = HEAD_DIM + ONES_ROWS
HEADS_PER_UNIT = 2
N_UNITS = H_B // HEADS_PER_UNIT
WIN_SPAN = WINDOW + Q_BLOCK

F32 = jnp.float32
BF16 = jnp.bfloat16


def _gelu_tanh(x):
    c = np.float32(np.sqrt(2.0 / np.pi))
    return x * (0.5 * (1.0 + jnp.tanh(c * (x + 0.044715 * (x * x * x)))))


def _rms_scale(x):
    return lax.rsqrt(jnp.mean(x * x, axis=-1, keepdims=True) + EPS)


def _dot(a, b):
    return jnp.dot(a, b, preferred_element_type=F32)


def _dot_nt(a, b):
    return lax.dot_general(a, b, (((1,), (1,)), ((), ())), preferred_element_type=F32)


def _in_proj_kernel(x_ref, nw_ref, wuv_ref, wq_ref, wc_ref, wkv_ref, wg_ref,
                    uv_ref, q_ref, c_ref, kv_ref, g_ref):
    x = x_ref[...]
    h = (x * _rms_scale(x) * nw_ref[...]).astype(BF16)
    uv_ref[...] = _dot(h, wuv_ref[...])
    q_ref[...] = (_dot(h, wq_ref[...]) * Q_SCALE).astype(BF16)
    c_ref[...] = _dot(h, wc_ref[...])
    kv_ref[...] = _dot(h, wkv_ref[...]).astype(BF16)
    g_ref[...] = _dot(h, wg_ref[...])


def _in_proj(x2d, nw, wuv, wq, wc, wkv, wg):
    t = x2d.shape[0]
    const = lambda i: (0, 0)
    row = lambda i: (i, 0)
    return pl.pallas_call(
        _in_proj_kernel,
        grid=(t // TM_PROJ,),
        in_specs=[
            pl.BlockSpec((TM_PROJ, D_MODEL), row),
            pl.BlockSpec((1, D_MODEL), const),
            pl.BlockSpec((D_MODEL, 2 * D_A), const),
            pl.BlockSpec((D_MODEL, D_B), const),
            pl.BlockSpec((D_MODEL, 2 * D_KV), const),
            pl.BlockSpec((D_MODEL, 4 * D_KV), const),
            pl.BlockSpec((D_MODEL, LANES), const),
        ],
        out_specs=[
            pl.BlockSpec((TM_PROJ, 2 * D_A), row),
            pl.BlockSpec((TM_PROJ, D_B), row),
            pl.BlockSpec((TM_PROJ, 2 * D_KV), row),
            pl.BlockSpec((TM_PROJ, 4 * D_KV), row),
            pl.BlockSpec((TM_PROJ, LANES), row),
        ],
        out_shape=[
            jax.ShapeDtypeStruct((t, 2 * D_A), F32),
            jax.ShapeDtypeStruct((t, D_B), BF16),
            jax.ShapeDtypeStruct((t, 2 * D_KV), F32),
            jax.ShapeDtypeStruct((t, 4 * D_KV), BF16),
            jax.ShapeDtypeStruct((t, LANES), F32),
        ],
        compiler_params=pltpu.CompilerParams(
            dimension_semantics=("parallel",), vmem_limit_bytes=VMEM_LIMIT_BYTES),
        name="in_proj",
    )(x2d, nw, wuv, wq, wc, wkv, wg)


def _gmlp_kernel(u_ref, v_ref, nw_ref, ws_ref, bst_ref, onw_ref, o_ref):
    row = lax.broadcasted_iota(jnp.int32, (CHUNK, CHUNK), 0)
    col = lax.broadcasted_iota(jnp.int32, (CHUNK, CHUNK), 1)
    causal = col <= row
    low_half = col < HEAD_DIM
    bst = bst_ref[...]
    w = [jnp.where(causal, ws_ref[hh], 0.0).astype(BF16) for hh in range(H_A)]
    for ci in range(GMLP_CHUNKS):
        rows = slice(ci * CHUNK, (ci + 1) * CHUNK)
        u = _gelu_tanh(u_ref[0, rows, :])
        v = _gelu_tanh(v_ref[0, rows, :])
        vn = (v * _rms_scale(v) * nw_ref[...]).astype(BF16)
        tiles = []
        for i in range(D_A // LANES):
            vt = vn[:, i * LANES:(i + 1) * LANES]
            pair = [_dot(w[hh], vt) + bst[:, hh:hh + 1] for hh in (2 * i, 2 * i + 1)]
            tiles.append(jnp.where(low_half, pair[0], pair[1]))
        a = u * jnp.concatenate(tiles, axis=1)
        o_ref[0, rows, :] = (a * _rms_scale(a) * onw_ref[...]).astype(BF16)


def _gmlp(uv3, nw, ws, bst, onw):
    b, s, _ = uv3.shape
    rows = GMLP_CHUNKS * CHUNK
    const2 = lambda bi, ci: (0, 0)
    return pl.pallas_call(
        _gmlp_kernel,
        grid=(b, s // rows),
        in_specs=[
            pl.BlockSpec((1, rows, D_A), lambda bi, ci: (bi, ci, 0)),
            pl.BlockSpec((1, rows, D_A), lambda bi, ci: (bi, ci, 1)),
            pl.BlockSpec((1, D_A), const2),
            pl.BlockSpec((H_A, CHUNK, CHUNK), lambda bi, ci: (0, 0, 0)),
            pl.BlockSpec((CHUNK, LANES), const2),
            pl.BlockSpec((1, D_A), const2),
        ],
        out_specs=pl.BlockSpec((1, rows, D_A), lambda bi, ci: (bi, ci, 0)),
        out_shape=jax.ShapeDtypeStruct((b, s, D_A), BF16),
        compiler_params=pltpu.CompilerParams(
            dimension_semantics=("parallel", "parallel"), vmem_limit_bytes=VMEM_LIMIT_BYTES),
        name="gmlp",
    )(uv3, uv3, nw, ws, bst, onw)


def _compress_kernel(x_ref, pos_ref, w1_ref, w2_ref, o_ref, h1_ref, h2_ref):
    n_grp = o_ref.shape[2]
    for r in range(CMP_STRIDE):
        xr = x_ref[0, pl.ds(r, n_grp, stride=CMP_STRIDE), :]
        lo = _dot((xr + pos_ref[0, r:r + 1, :]).astype(BF16), w1_ref[0, r])
        hi = _dot((xr + pos_ref[0, CMP_STRIDE + r:CMP_STRIDE + r + 1, :]).astype(BF16),
                  w1_ref[0, CMP_STRIDE + r])
        if r == 0:
            h1_ref[...] = lo
            h2_ref[...] = hi
        else:
            h1_ref[...] += lo
            h2_ref[...] += hi
    h2_next = pltpu.roll(h2_ref[...], n_grp - 1, 0)
    hid = _gelu_tanh(h1_ref[...] + h2_next).astype(BF16)
    o_ref[0, 0] = _dot(hid, w2_ref[0]).astype(BF16)


def _compress(x3, pos, w1, w2):
    b, s, _ = x3.shape
    n_grp = s // CMP_STRIDE
    return pl.pallas_call(
        _compress_kernel,
        grid=(b, 2),
        in_specs=[
            pl.BlockSpec((1, s, D_KV), lambda bi, ti: (bi, 0, ti)),
            pl.BlockSpec((1,) + pos.shape[1:], lambda bi, ti: (ti, 0, 0)),
            pl.BlockSpec((1,) + w1.shape[1:], lambda bi, ti: (ti, 0, 0, 0)),
            pl.BlockSpec((1,) + w2.shape[1:], lambda bi, ti: (ti, 0, 0)),
        ],
        out_specs=pl.BlockSpec((1, 1, n_grp, D_KV), lambda bi, ti: (bi, ti, 0, 0)),
        out_shape=jax.ShapeDtypeStruct((b, 2, n_grp, D_KV), BF16),
        scratch_shapes=[pltpu.VMEM((n_grp, N_KV * CMP_HIDDEN), F32)] * 2,
        compiler_params=pltpu.CompilerParams(
            dimension_semantics=("parallel", "parallel"), vmem_limit_bytes=VMEM_LIMIT_BYTES),
        name="compress",
    )(x3, pos, w1, w2)


def _per_kv_head_blockdiag(w):
    k, n = w.shape[-2:]
    out = jnp.zeros(w.shape[:-2] + (N_KV * k, N_KV * n), w.dtype)
    for kh in range(N_KV):
        out = out.at[..., kh * k:(kh + 1) * k, kh * n:(kh + 1) * n].set(w)
    return out


S_BLOCKS = 64


def _softmax_cols(s, mask):
    s = jnp.where(mask, s, NEG)
    return jnp.exp2(s - jnp.max(s, axis=0, keepdims=True)).astype(BF16)


def _own_lanes(lane, kh):
    return (lane >= kh * HEAD_DIM) & (lane < (kh + 1) * HEAD_DIM)


def _spare_lane0(kh):
    return (1 - kh) * HEAD_DIM


def _normalize(r):
    return r[:HEAD_DIM] * (1.0 / r[HEAD_DIM:HEAD_DIM + 1])


def _nsa_kernel(q_ref, kc_ref, vc_ref, ks_ref, vs_ref, kw_ref, vw_ref, g_ref, gb_ref,
                onw_ref, o_ref, ks_aug_ref, vs_t_ref, vw_t_ref, vc_t_ref, psum_ref,
                s_a_ref, s_b_ref, p_a_ref, p_b_ref):
    c = pl.program_id(1)
    t0 = c * Q_BLOCK
    n_cmp_pad = kc_ref.shape[1]
    seq = ks_ref.shape[1]
    unit_w = HEADS_PER_UNIT * Q_BLOCK

    @pl.when(c == 0)
    def _prepare_batch_row():
        lane = lax.broadcasted_iota(jnp.int32, (LANES, LANES), 1)
        blk_of_row = lax.broadcasted_iota(jnp.int32, (LANES, LANES), 0) // SEL_BLOCK

        def prep(i, carry):
            r0 = pl.multiple_of(i * LANES, LANES)
            keys = ks_ref[0, pl.ds(r0, LANES), :]
            key_blk = blk_of_row + i * (LANES // SEL_BLOCK)
            for kh in range(N_KV):
                one_hot = jnp.where(lane - _spare_lane0(kh) == key_blk, 1.0, 0.0).astype(BF16)
                ks_aug_ref[kh, pl.ds(r0, LANES), :] = jnp.where(_own_lanes(lane, kh), keys, one_hot)
            for src, dst in ((vs_ref, vs_t_ref), (vw_ref, vw_t_ref)):
                chunk_t = src[0, pl.ds(r0, LANES), :].astype(F32).T.astype(BF16)
                for kh in range(N_KV):
                    dst[i, kh, :HEAD_DIM] = chunk_t[kh * HEAD_DIM:(kh + 1) * HEAD_DIM]
                    dst[i, kh, HEAD_DIM:] = jnp.ones((ONES_ROWS, LANES), BF16)
            return carry
        lax.fori_loop(0, seq // LANES, prep, 0)
        for i in range(n_cmp_pad // LANES):
            vc_t_ref[i] = vc_ref[0, i * LANES:(i + 1) * LANES, :].astype(F32).T.astype(BF16)

    q_lane = lax.broadcasted_iota(jnp.int32, (Q_BLOCK, LANES), 1)

    def placed(hh):
        tile = q_ref[0, :, (hh // 2) * LANES:(hh // 2 + 1) * LANES].astype(F32)
        kh = hh // GQA
        if hh % 2 != kh:
            tile = pltpu.roll(tile, HEAD_DIM, 1)
        return jnp.where(_own_lanes(q_lane, kh), tile, 0.0).astype(BF16)

    q_unit = [jnp.concatenate([placed(hh)
                               for hh in range(u * HEADS_PER_UNIT, (u + 1) * HEADS_PER_UNIT)],
                              axis=0) for u in range(N_UNITS)]
    unit_kv = [u * HEADS_PER_UNIT // GQA for u in range(N_UNITS)]
    gates_t = jax.nn.sigmoid(g_ref[0] + gb_ref[...]).T

    def q_pos(rows):
        return t0 + (lax.broadcasted_iota(jnp.int32, (rows, unit_w), 1) & (Q_BLOCK - 1))

    def value_products(v_ref, chunk0, n_chunks, p_of_unit):
        v_aug = [jnp.concatenate([v_ref[chunk0 + j, kh] for j in range(n_chunks)], axis=1)
                 if n_chunks > 1 else v_ref[chunk0, kh] for kh in range(N_KV)]
        return [_dot(v_aug[unit_kv[u]], p_of_unit(u)) for u in range(N_UNITS)]

    w0 = pl.multiple_of(jnp.maximum(t0 - WINDOW, 0), Q_BLOCK)
    kc = kc_ref[0]
    kw = kw_ref[0, pl.ds(w0, WIN_SPAN), :]
    kd = ks_ref[0, pl.ds(pl.multiple_of(t0, Q_BLOCK), Q_BLOCK), :]
    scores_c = [_dot_nt(kc, q_unit[u]) for u in range(N_UNITS)]
    scores_w = [_dot_nt(kw, q_unit[u]) for u in range(N_UNITS)]
    scores_d = [_dot_nt(kd, q_unit[u]) for u in range(N_UNITS)]

    n_idx = lax.broadcasted_iota(jnp.int32, (n_cmp_pad, unit_w), 0)
    cmp_mask = (n_idx * CMP_STRIDE + (CMP_BLOCK - 1)) <= q_pos(n_cmp_pad)
    sees_a_block = q_pos(1) >= CMP_BLOCK - 1
    p_c = []
    p_sum = [None] * N_KV
    for u in range(N_UNITS):
        kh = unit_kv[u]
        s = jnp.where(cmp_mask, scores_c[u], NEG)
        m = jnp.max(s, axis=0, keepdims=True)
        p = jnp.exp2(s - m)
        l = jnp.sum(p, axis=0, keepdims=True)
        p = p * jnp.where(sees_a_block, 1.0 / l, 0.0)
        p_c.append(p.astype(BF16))
        for hh in range(HEADS_PER_UNIT):
            ph = p[:, hh * Q_BLOCK:(hh + 1) * Q_BLOCK]
            p_sum[kh] = ph if p_sum[kh] is None else p_sum[kh] + ph

    diff = q_pos(WIN_SPAN) - (w0 + lax.broadcasted_iota(jnp.int32, (WIN_SPAN, unit_w), 0))
    win_mask = (diff >= 0) & (diff < WINDOW)
    p_w = [_softmax_cols(scores_w[u], win_mask) for u in range(N_UNITS)]

    key_d = lax.broadcasted_iota(jnp.int32, (Q_BLOCK, unit_w), 0)
    causal_d = (t0 + key_d) <= q_pos(Q_BLOCK)
    m_d, p_d = [], []
    for u in range(N_UNITS):
        s = jnp.where(causal_d, scores_d[u], NEG)
        m_d.append(jnp.max(s, axis=0, keepdims=True))
        p_d.append(jnp.exp2(s - m_d[u]).astype(BF16))

    vc_t = jnp.concatenate([vc_t_ref[i] for i in range(n_cmp_pad // LANES)], axis=1)
    o_c = [_dot(vc_t[unit_kv[u] * HEAD_DIM:(unit_kv[u] + 1) * HEAD_DIM], p_c[u])
           for u in range(N_UNITS)]
    r_w = value_products(vw_t_ref, w0 // LANES, WIN_SPAN // LANES, lambda u: p_w[u])
    acc_d = value_products(vs_t_ref, c, 1, lambda u: p_d[u])

    ratio = SEL_BLOCK // CMP_STRIDE
    span = CMP_BLOCK // CMP_STRIDE
    j_idx = lax.broadcasted_iota(jnp.int32, (SUBLANES, Q_BLOCK), 0)
    cur = (t0 + lax.broadcasted_iota(jnp.int32, (SUBLANES, Q_BLOCK), 1)) // SEL_BLOCK
    first_own_block = t0 // SEL_BLOCK
    n_grp = S_BLOCKS // SUBLANES
    q_aug = [None] * N_UNITS
    for kh in range(N_KV):
        psum_ref[kh, :SUBLANES, :] = jnp.zeros((SUBLANES, Q_BLOCK), F32)
        psum_ref[kh, SUBLANES:, :] = p_sum[kh]
        imp_t = psum_ref[kh, pl.ds(SUBLANES - (span - 1), S_BLOCKS, stride=ratio), :]
        for k in range(2 - span, ratio):
            imp_t = imp_t + psum_ref[kh, pl.ds(SUBLANES + k, S_BLOCKS, stride=ratio), :]
        score = []
        for r in range(n_grp):
            j = j_idx + r * SUBLANES
            forced = (j == 0) | (j == cur) | (j == cur - 1)
            sc = jnp.where(j <= cur, imp_t[r * SUBLANES:(r + 1) * SUBLANES], -FORCE)
            score.append(jnp.where(forced, FORCE, sc))
        bias_rows = []
        for r in range(n_grp):
            cnt = jnp.zeros((SUBLANES, Q_BLOCK), jnp.int32)
            for jp in range(S_BLOCKS):
                rp = jp // SUBLANES
                other = score[rp][jp % SUBLANES:jp % SUBLANES + 1, :]
                ge = (other >= score[r]).astype(jnp.int32)
                gt = (other > score[r]).astype(jnp.int32)
                if rp < r:
                    ahead = ge
                elif rp > r:
                    ahead = gt
                else:
                    ahead = jnp.where(j_idx > (jp % SUBLANES), ge, gt)
                cnt = cnt + ahead
            past = (j_idx + r * SUBLANES) < first_own_block
            bias_rows.append(jnp.where(past, jnp.where(cnt < N_SELECT, 0.0, NEG), NEG))
        zero_rows = [jnp.zeros((LANES - S_BLOCKS, Q_BLOCK), F32)]
        bias_t = jnp.concatenate(
            zero_rows + bias_rows if _spare_lane0(kh) else bias_rows + zero_rows, axis=0)
        bias_q = jnp.concatenate([bias_t.T.astype(BF16)] * HEADS_PER_UNIT, axis=0)
        own = _own_lanes(lax.broadcasted_iota(jnp.int32, bias_q.shape, 1), kh)
        for u in range(N_UNITS):
            if unit_kv[u] == kh:
                q_aug[u] = jnp.where(own, q_unit[u], bias_q)

    o_w = [_normalize(r) for r in r_w]

    chunks = SEL_TILE // LANES
    last_tile = seq // SEL_TILE - 1

    def score_tile(tile, s_ref):
        k0 = pl.multiple_of(tile * SEL_TILE, SEL_TILE)
        kt = [ks_aug_ref[kh, pl.ds(k0, SEL_TILE), :] for kh in range(N_KV)]
        for u in range(N_UNITS):
            s_ref[u] = _dot_nt(kt[unit_kv[u]], q_aug[u])

    def stage(tile, s_cur, s_next, p_cur, p_prev, states, prefetch=True):
        pv_prev = value_products(vs_t_ref, jnp.maximum(tile - 1, 0) * chunks, chunks,
                                 lambda u: p_prev[u])
        if prefetch:
            score_tile(jnp.minimum(tile + 1, last_tile), s_next)
        out = []
        for u in range(N_UNITS):
            m, alpha_prev, acc = states[u]
            s = s_cur[u]
            m_new = jnp.maximum(m, jnp.max(s, axis=0, keepdims=True))
            p_cur[u] = jnp.exp2(s - m_new).astype(BF16)
            out.append((m_new, jnp.exp2(m - m_new), alpha_prev * acc + pv_prev[u]))
        return tuple(out)

    def trip(first_tile, states, prefetch_after=True):
        states = stage(first_tile, s_a_ref, s_b_ref, p_a_ref, p_b_ref, states)
        return stage(first_tile + 1, s_b_ref, s_a_ref, p_b_ref, p_a_ref, states, prefetch_after)

    blocks_per_trip = 2 * SEL_TILE // Q_BLOCK
    loop_trips = jnp.maximum((c + blocks_per_trip - 1) // blocks_per_trip - 1, 0)
    score_tile(0, s_a_ref)
    p_b_ref[...] = jnp.zeros(p_b_ref.shape, BF16)
    init = tuple((m_d[u], jnp.ones((1, unit_w), F32), acc_d[u]) for u in range(N_UNITS))
    sel_states = lax.fori_loop(0, loop_trips, lambda j, st: trip(2 * j, st), init)
    sel_states = trip(2 * loop_trips, sel_states, prefetch_after=False)
    pv_last = value_products(vs_t_ref, (2 * loop_trips + 1) * chunks, chunks,
                             lambda u: p_b_ref[u])
    o_s = [_normalize(alpha * acc + pv_last[u]) for u, (_, alpha, acc) in enumerate(sel_states)]

    mixed = []
    for hh in range(H_B):
        u, sl = hh // HEADS_PER_UNIT, slice((hh % HEADS_PER_UNIT) * Q_BLOCK,
                                            (hh % HEADS_PER_UNIT + 1) * Q_BLOCK)
        mixed.append(gates_t[hh:hh + 1] * o_c[u][:, sl]
                     + gates_t[H_B + hh:H_B + hh + 1] * o_s[u][:, sl]
                     + gates_t[2 * H_B + hh:2 * H_B + hh + 1] * o_w[u][:, sl])
    bmix = jnp.concatenate(mixed, axis=0).T
    o_ref[0] = (bmix * _rms_scale(bmix) * onw_ref[...]).astype(BF16)


def _nsa(q3, kcv, kv3, g3, gate_b, onw):
    b, s, _ = q3.shape
    n_cmp_pad = kcv.shape[2]
    unit_w = HEADS_PER_UNIT * Q_BLOCK
    const2 = lambda bi, ci: (0, 0)
    kv_spec = lambda col: pl.BlockSpec((1, s, D_KV), lambda bi, ci: (bi, 0, col))
    return pl.pallas_call(
        _nsa_kernel,
        grid=(b, s // Q_BLOCK),
        in_specs=[
            pl.BlockSpec((1, Q_BLOCK, D_B), lambda bi, ci: (bi, ci, 0)),
            pl.BlockSpec((None, 1, n_cmp_pad, D_KV), lambda bi, ci: (bi, 0, 0, 0)),
            pl.BlockSpec((None, 1, n_cmp_pad, D_KV), lambda bi, ci: (bi, 1, 0, 0)),
            kv_spec(0), kv_spec(1), kv_spec(2), kv_spec(3),
            pl.BlockSpec((1, Q_BLOCK, LANES), lambda bi, ci: (bi, ci, 0)),
            pl.BlockSpec((1, LANES), const2),
            pl.BlockSpec((1, D_B), const2),
        ],
        out_specs=pl.BlockSpec((1, Q_BLOCK, D_B), lambda bi, ci: (bi, ci, 0)),
        out_shape=jax.ShapeDtypeStruct((b, s, D_B), BF16),
        scratch_shapes=[
            pltpu.VMEM((N_KV, s, LANES), BF16),
            pltpu.VMEM((s // LANES, N_KV, V_ROWS, LANES), BF16),
            pltpu.VMEM((s // LANES, N_KV, V_ROWS, LANES), BF16),
            pltpu.VMEM((n_cmp_pad // LANES, D_KV, LANES), BF16),
            pltpu.VMEM((N_KV, SUBLANES + n_cmp_pad, Q_BLOCK), F32),
            pltpu.VMEM((N_UNITS, SEL_TILE, unit_w), F32),
            pltpu.VMEM((N_UNITS, SEL_TILE, unit_w), F32),
            pltpu.VMEM((N_UNITS, SEL_TILE, unit_w), BF16),
            pltpu.VMEM((N_UNITS, SEL_TILE, unit_w), BF16),
        ],
        compiler_params=pltpu.CompilerParams(
            dimension_semantics=("arbitrary", "arbitrary"), vmem_limit_bytes=VMEM_LIMIT_BYTES),
        name="nsa",
    )(q3, kcv, kcv, kv3, kv3, kv3, kv3, g3, gate_b, onw)


def _mix_ffn_kernel(x_ref, a_ref, b_ref, nw_ref, fnw_ref, woa_hbm, wob_hbm, wg_hbm, wu_hbm,
                    wd_hbm, o_ref, woa_ref, wob_ref, wg_ref, wu_ref, wd_ref, sem, *, final_norm):
    @pl.when(pl.program_id(0) == 0)
    def _load_weights():
        pairs = ((woa_hbm, woa_ref), (wob_hbm, wob_ref), (wg_hbm, wg_ref), (wu_hbm, wu_ref),
                 (wd_hbm, wd_ref))
        copies = [pltpu.make_async_copy(src, dst, sem.at[i]) for i, (src, dst) in enumerate(pairs)]
        for cp in copies:
            cp.start()
        for cp in copies:
            cp.wait()

    x1 = x_ref[...] + _dot(a_ref[...], woa_ref[...]) + _dot(b_ref[...], wob_ref[...])
    h = (x1 * _rms_scale(x1) * nw_ref[...]).astype(BF16)
    gate = _dot(h, wg_ref[...])
    up = _dot(h, wu_ref[...])
    act = (gate * jax.nn.sigmoid(gate) * up).astype(BF16)
    acc = x1 + _dot(act, wd_ref[...])
    if final_norm:
        acc = acc * _rms_scale(acc) * fnw_ref[...]
    o_ref[...] = acc


def _mix_ffn(x2d, a2d, b2d, woa, wob, nw, wg, wu, wd, fnw, final_norm):
    t = x2d.shape[0]
    row = lambda i: (i, 0)
    const = lambda i: (0, 0)
    weights = (woa, wob, wg, wu, wd)
    in_hbm = pl.BlockSpec(memory_space=pl.ANY)
    return pl.pallas_call(
        functools.partial(_mix_ffn_kernel, final_norm=final_norm),
        grid=(t // TM_FFN,),
        in_specs=[
            pl.BlockSpec((TM_FFN, D_MODEL), row),
            pl.BlockSpec((TM_FFN, D_A), row),
            pl.BlockSpec((TM_FFN, D_B), row),
            pl.BlockSpec((1, D_MODEL), const),
            pl.BlockSpec((1, D_MODEL), const),
        ] + [in_hbm] * len(weights),
        out_specs=pl.BlockSpec((TM_FFN, D_MODEL), row),
        out_shape=jax.ShapeDtypeStruct((t, D_MODEL), F32),
        scratch_shapes=[pltpu.VMEM(wt.shape, wt.dtype) for wt in weights]
        + [pltpu.SemaphoreType.DMA((len(weights),))],
        compiler_params=pltpu.CompilerParams(
            dimension_semantics=("arbitrary",), vmem_limit_bytes=VMEM_LIMIT_BYTES),
        name="mix_ffn",
    )(x2d, a2d, b2d, nw, fnw, *weights)


def kernel(x, norm_mix_w, w_in, gmlp_norm_w, gmlp_ws, gmlp_bs, cmp_pos_k, cmp_pos_v, cmp_k_w1,
           cmp_k_w2, cmp_v_w1, cmp_v_w2, gate_b, out_norm_a_w, out_norm_b_w, w_o, norm_ffn_w,
           w_gate, w_up, w_down, final_norm_w):
    b, s, d = x.shape
    assert d == D_MODEL and s // SEL_BLOCK == S_BLOCKS and s % Q_BLOCK == 0
    assert (b * s) % TM_PROJ == 0 and (b * s) % TM_FFN == 0 and s % (GMLP_CHUNKS * CHUNK) == 0
    t = b * s
    o_uv, o_q = 2 * D_A, 2 * D_A + D_B
    o_c, o_kv = o_q + 2 * D_KV, o_q + 6 * D_KV

    x2d = x.reshape(t, d)
    for l in range(DEPTH):
        w = w_in[l]
        wg_pad = jnp.pad(w[:, o_kv:], ((0, 0), (0, LANES - N_GATES * H_B)))
        uv, q, kc_raw, kv, g_raw = _in_proj(
            x2d, norm_mix_w[l][None, :], w[:, :o_uv].astype(BF16), w[:, o_uv:o_q].astype(BF16),
            w[:, o_q:o_c].astype(BF16), w[:, o_c:o_kv].astype(BF16), wg_pad.astype(BF16))

        bst = jnp.pad(gmlp_bs[l].T, ((0, 0), (0, LANES - H_A)))
        a_n = _gmlp(uv.reshape(b, s, 2 * D_A), gmlp_norm_w[l][None, :], gmlp_ws[l], bst,
                    out_norm_a_w[l][None, :])

        kv3 = kv.reshape(b, s, 4 * D_KV)
        pos = jnp.tile(jnp.stack([cmp_pos_k[l], cmp_pos_v[l]]), (1, 1, N_KV))
        w1 = jnp.stack([cmp_k_w1[l], cmp_v_w1[l]]).reshape(2, CMP_BLOCK, HEAD_DIM, CMP_HIDDEN)
        w1 = _per_kv_head_blockdiag(w1).astype(BF16)
        w2 = _per_kv_head_blockdiag(jnp.stack([cmp_k_w2[l], cmp_v_w2[l]])).astype(BF16)
        kcv = _compress(kc_raw.reshape(b, s, 2 * D_KV), pos, w1, w2)

        gb_pad = jnp.pad(gate_b[l], (0, LANES - N_GATES * H_B))[None, :]
        b_n = _nsa(q.reshape(b, s, D_B), kcv, kv3, g_raw.reshape(b, s, LANES), gb_pad,
                   out_norm_b_w[l][None, :])

        x2d = _mix_ffn(
            x2d, a_n.reshape(t, D_A), b_n.reshape(t, D_B),
            w_o[l][:D_A].astype(BF16), w_o[l][D_A:].astype(BF16), norm_ffn_w[l][None, :],
            w_gate[l].astype(BF16), w_up[l].astype(BF16), w_down[l].astype(BF16),
            final_norm_w[None, :], final_norm=(l == DEPTH - 1))
    return x2d.reshape(b, s, d)
```

```python
import functools

import jax
import jax.numpy as jnp
import numpy as np
from jax import lax
from jax.experimental import pallas as pl
from jax.experimental.pallas import tpu as pltpu

D_MODEL = 1024
DEPTH = 2
D_A = 512
D_B = 512
HEAD_DIM = 64
H_A = D_A // HEAD_DIM
H_B = D_B // HEAD_DIM
N_KV = 2
GQA = H_B // N_KV
D_KV = N_KV * HEAD_DIM
CHUNK = 128
CMP_BLOCK = 32
CMP_STRIDE = 16
CMP_HIDDEN = 256
SEL_BLOCK = 64
N_SELECT = 16
WINDOW = 512
Q_BLOCK = 128
N_GATES = 3
D_FF = 2816
EPS = 1e-6
NEG = -1e30
FORCE = 1e4
Q_SCALE = HEAD_DIM ** -0.5 * float(np.log2(np.e))

LANES = 128
SUBLANES = 8
VMEM_LIMIT_BYTES = 56 * 1024 * 1024

TM_PROJ = 1024
TM_FFN = 512
GMLP_CHUNKS = 8
SEL_TILE = 256
ONES_ROWS = 16
V_ROWS = HEAD_DIM + ONES_ROWS
HEADS_PER_UNIT = 2
N_UNITS = H_B // HEADS_PER_UNIT
WIN_SPAN = WINDOW + Q_BLOCK

F32 = jnp.float32
BF16 = jnp.bfloat16


def _gelu_tanh(x):
    c = np.float32(np.sqrt(2.0 / np.pi))
    return x * (0.5 * (1.0 + jnp.tanh(c * (x + 0.044715 * (x * x * x)))))


def _rms_scale(x):
    return lax.rsqrt(jnp.mean(x * x, axis=-1, keepdims=True) + EPS)


def _dot(a, b):
    return jnp.dot(a, b, preferred_element_type=F32)


def _dot_nt(a, b):
    return lax.dot_general(a, b, (((1,), (1,)), ((), ())), preferred_element_type=F32)


def _in_proj_kernel(x_ref, nw_ref, wuv_ref, wq_ref, wc_ref, wkv_ref, wg_ref,
                    uv_ref, q_ref, c_ref, kv_ref, g_ref):
    x = x_ref[...]
    h = (x * _rms_scale(x) * nw_ref[...]).astype(BF16)
    uv_ref[...] = _dot(h, wuv_ref[...])
    q_ref[...] = (_dot(h, wq_ref[...]) * Q_SCALE).astype(BF16)
    c_ref[...] = _dot(h, wc_ref[...])
    kv_ref[...] = _dot(h, wkv_ref[...]).astype(BF16)
    g_ref[...] = _dot(h, wg_ref[...])


def _in_proj(x2d, nw, wuv, wq, wc, wkv, wg):
    t = x2d.shape[0]
    const = lambda i: (0, 0)
    row = lambda i: (i, 0)
    return pl.pallas_call(
        _in_proj_kernel,
        grid=(t // TM_PROJ,),
        in_specs=[
            pl.BlockSpec((TM_PROJ, D_MODEL), row),
            pl.BlockSpec((1, D_MODEL), const),
            pl.BlockSpec((D_MODEL, 2 * D_A), const),
            pl.BlockSpec((D_MODEL, D_B), const),
            pl.BlockSpec((D_MODEL, 2 * D_KV), const),
            pl.BlockSpec((D_MODEL, 4 * D_KV), const),
            pl.BlockSpec((D_MODEL, LANES), const),
        ],
        out_specs=[
            pl.BlockSpec((TM_PROJ, 2 * D_A), row),
            pl.BlockSpec((TM_PROJ, D_B), row),
            pl.BlockSpec((TM_PROJ, 2 * D_KV), row),
            pl.BlockSpec((TM_PROJ, 4 * D_KV), row),
            pl.BlockSpec((TM_PROJ, LANES), row),
        ],
        out_shape=[
            jax.ShapeDtypeStruct((t, 2 * D_A), F32),
            jax.ShapeDtypeStruct((t, D_B), BF16),
            jax.ShapeDtypeStruct((t, 2 * D_KV), F32),
            jax.ShapeDtypeStruct((t, 4 * D_KV), BF16),
            jax.ShapeDtypeStruct((t, LANES), F32),
        ],
        compiler_params=pltpu.CompilerParams(
            dimension_semantics=("parallel",), vmem_limit_bytes=VMEM_LIMIT_BYTES),
        name="in_proj",
    )(x2d, nw, wuv, wq, wc, wkv, wg)


def _gmlp_kernel(u_ref, v_ref, nw_ref, ws_ref, bst_ref, onw_ref, o_ref):
    row = lax.broadcasted_iota(jnp.int32, (CHUNK, CHUNK), 0)
    col = lax.broadcasted_iota(jnp.int32, (CHUNK, CHUNK), 1)
    causal = col <= row
    low_half = col < HEAD_DIM
    bst = bst_ref[...]
    w = [jnp.where(causal, ws_ref[hh], 0.0).astype(BF16) for hh in range(H_A)]
    for ci in range(GMLP_CHUNKS):
        rows = slice(ci * CHUNK, (ci + 1) * CHUNK)
        u = _gelu_tanh(u_ref[0, rows, :])
        v = _gelu_tanh(v_ref[0, rows, :])
        vn = (v * _rms_scale(v) * nw_ref[...]).astype(BF16)
        tiles = []
        for i in range(D_A // LANES):
            vt = vn[:, i * LANES:(i + 1) * LANES]
            pair = [_dot(w[hh], vt) + bst[:, hh:hh + 1] for hh in (2 * i, 2 * i + 1)]
            tiles.append(jnp.where(low_half, pair[0], pair[1]))
        a = u * jnp.concatenate(tiles, axis=1)
        o_ref[0, rows, :] = (a * _rms_scale(a) * onw_ref[...]).astype(BF16)


def _gmlp(uv3, nw, ws, bst, onw):
    b, s, _ = uv3.shape
    rows = GMLP_CHUNKS * CHUNK
    const2 = lambda bi, ci: (0, 0)
    return pl.pallas_call(
        _gmlp_kernel,
        grid=(b, s // rows),
        in_specs=[
            pl.BlockSpec((1, rows, D_A), lambda bi, ci: (bi, ci, 0)),
            pl.BlockSpec((1, rows, D_A), lambda bi, ci: (bi, ci, 1)),
            pl.BlockSpec((1, D_A), const2),
            pl.BlockSpec((H_A, CHUNK, CHUNK), lambda bi, ci: (0, 0, 0)),
            pl.BlockSpec((CHUNK, LANES), const2),
            pl.BlockSpec((1, D_A), const2),
        ],
        out_specs=pl.BlockSpec((1, rows, D_A), lambda bi, ci: (bi, ci, 0)),
        out_shape=jax.ShapeDtypeStruct((b, s, D_A), BF16),
        compiler_params=pltpu.CompilerParams(
            dimension_semantics=("parallel", "parallel"), vmem_limit_bytes=VMEM_LIMIT_BYTES),
        name="gmlp",
    )(uv3, uv3, nw, ws, bst, onw)


def _compress_kernel(x_ref, pos_ref, w1_ref, w2_ref, o_ref, h1_ref, h2_ref):
    n_grp = o_ref.shape[2]
    for r in range(CMP_STRIDE):
        xr = x_ref[0, pl.ds(r, n_grp, stride=CMP_STRIDE), :]
        lo = _dot((xr + pos_ref[0, r:r + 1, :]).astype(BF16), w1_ref[0, r])
        hi = _dot((xr + pos_ref[0, CMP_STRIDE + r:CMP_STRIDE + r + 1, :]).astype(BF16),
                  w1_ref[0, CMP_STRIDE + r])
        if r == 0:
            h1_ref[...] = lo
            h2_ref[...] = hi
        else:
            h1_ref[...] += lo
            h2_ref[...] += hi
    h2_next = pltpu.roll(h2_ref[...], n_grp - 1, 0)
    hid = _gelu_tanh(h1_ref[...] + h2_next).astype(BF16)
    o_ref[0, 0] = _dot(hid, w2_ref[0]).astype(BF16)


def _compress(x3, pos, w1, w2):
    b, s, _ = x3.shape
    n_grp = s // CMP_STRIDE
    return pl.pallas_call(
        _compress_kernel,
        grid=(b, 2),
        in_specs=[
            pl.BlockSpec((1, s, D_KV), lambda bi, ti: (bi, 0, ti)),
            pl.BlockSpec((1,) + pos.shape[1:], lambda bi, ti: (ti, 0, 0)),
            pl.BlockSpec((1,) + w1.shape[1:], lambda bi, ti: (ti, 0, 0, 0)),
            pl.BlockSpec((1,) + w2.shape[1:], lambda bi, ti: (ti, 0, 0)),
        ],
        out_specs=pl.BlockSpec((1, 1, n_grp, D_KV), lambda bi, ti: (bi, ti, 0, 0)),
        out_shape=jax.ShapeDtypeStruct((b, 2, n_grp, D_KV), BF16),
        scratch_shapes=[pltpu.VMEM((n_grp, N_KV * CMP_HIDDEN), F32)] * 2,
        compiler_params=pltpu.CompilerParams(
            dimension_semantics=("parallel", "parallel"), vmem_limit_bytes=VMEM_LIMIT_BYTES),
        name="compress",
    )(x3, pos, w1, w2)


def _per_kv_head_blockdiag(w):
    k, n = w.shape[-2:]
    out = jnp.zeros(w.shape[:-2] + (N_KV * k, N_KV * n), w.dtype)
    for kh in range(N_KV):
        out = out.at[..., kh * k:(kh + 1) * k, kh * n:(kh + 1) * n].set(w)
    return out


S_BLOCKS = 64
assert N_KV == 2 and S_BLOCKS <= LANES - HEAD_DIM


def _softmax_cols(s, mask):
    s = jnp.where(mask, s, NEG)
    return jnp.exp2(s - jnp.max(s, axis=0, keepdims=True)).astype(BF16)


def _own_lanes(lane, kh):
    return (lane >= kh * HEAD_DIM) & (lane < (kh + 1) * HEAD_DIM)


def _spare_lane0(kh):
    return (1 - kh) * HEAD_DIM


def _normalize(r):
    return r[:HEAD_DIM] * (1.0 / r[HEAD_DIM:HEAD_DIM + 1])


def _nsa_kernel(q_ref, kc_ref, vc_ref, ks_ref, vs_ref, kw_ref, vw_ref, g_ref, gb_ref,
                onw_ref, o_ref, ks_aug_ref, vs_t_ref, vw_t_ref, vc_t_ref, psum_ref,
                s_a_ref, s_b_ref, p_a_ref, p_b_ref):
    c = pl.program_id(1)
    t0 = c * Q_BLOCK
    n_cmp_pad = kc_ref.shape[1]
    seq = ks_ref.shape[1]
    unit_w = HEADS_PER_UNIT * Q_BLOCK

    @pl.when(c == 0)
    def _prepare_batch_row():
        lane = lax.broadcasted_iota(jnp.int32, (LANES, LANES), 1)
        blk_of_row = lax.broadcasted_iota(jnp.int32, (LANES, LANES), 0) // SEL_BLOCK

        def prep(i, carry):
            r0 = pl.multiple_of(i * LANES, LANES)
            keys = ks_ref[0, pl.ds(r0, LANES), :]
            key_blk = blk_of_row + i * (LANES // SEL_BLOCK)
            for kh in range(N_KV):
                one_hot = jnp.where(lane - _spare_lane0(kh) == key_blk, 1.0, 0.0).astype(BF16)
                ks_aug_ref[kh, pl.ds(r0, LANES), :] = jnp.where(_own_lanes(lane, kh), keys, one_hot)
            for src, dst in ((vs_ref, vs_t_ref), (vw_ref, vw_t_ref)):
                chunk_t = src[0, pl.ds(r0, LANES), :].astype(F32).T.astype(BF16)
                for kh in range(N_KV):
                    dst[i, kh, :HEAD_DIM] = chunk_t[kh * HEAD_DIM:(kh + 1) * HEAD_DIM]
                    dst[i, kh, HEAD_DIM:] = jnp.ones((ONES_ROWS, LANES), BF16)
            return carry
        lax.fori_loop(0, seq // LANES, prep, 0)
        for i in range(n_cmp_pad // LANES):
            vc_t_ref[i] = vc_ref[0, i * LANES:(i + 1) * LANES, :].astype(F32).T.astype(BF16)

    q_lane = lax.broadcasted_iota(jnp.int32, (Q_BLOCK, LANES), 1)

    def placed(hh):
        tile = q_ref[0, :, (hh // 2) * LANES:(hh // 2 + 1) * LANES].astype(F32)
        kh = hh // GQA
        if hh % 2 != kh:
            tile = pltpu.roll(tile, HEAD_DIM, 1)
        return jnp.where(_own_lanes(q_lane, kh), tile, 0.0).astype(BF16)

    q_unit = [jnp.concatenate([placed(hh)
                               for hh in range(u * HEADS_PER_UNIT, (u + 1) * HEADS_PER_UNIT)],
                              axis=0) for u in range(N_UNITS)]
    unit_kv = [u * HEADS_PER_UNIT // GQA for u in range(N_UNITS)]
    gates_t = jax.nn.sigmoid(g_ref[0] + gb_ref[...]).T

    def q_pos(rows):
        return t0 + (lax.broadcasted_iota(jnp.int32, (rows, unit_w), 1) & (Q_BLOCK - 1))

    def value_products(v_ref, chunk0, n_chunks, p_of_unit):
        v_aug = [jnp.concatenate([v_ref[chunk0 + j, kh] for j in range(n_chunks)], axis=1)
                 if n_chunks > 1 else v_ref[chunk0, kh] for kh in range(N_KV)]
        return [_dot(v_aug[unit_kv[u]], p_of_unit(u)) for u in range(N_UNITS)]

    w0 = pl.multiple_of(jnp.maximum(t0 - WINDOW, 0), Q_BLOCK)
    kc = kc_ref[0]
    kw = kw_ref[0, pl.ds(w0, WIN_SPAN), :]
    kd = ks_ref[0, pl.ds(pl.multiple_of(t0, Q_BLOCK), Q_BLOCK), :]
    scores_c = [_dot_nt(kc, q_unit[u]) for u in range(N_UNITS)]
    scores_w = [_dot_nt(kw, q_unit[u]) for u in range(N_UNITS)]
    scores_d = [_dot_nt(kd, q_unit[u]) for u in range(N_UNITS)]

    n_idx = lax.broadcasted_iota(jnp.int32, (n_cmp_pad, unit_w), 0)
    cmp_mask = (n_idx * CMP_STRIDE + (CMP_BLOCK - 1)) <= q_pos(n_cmp_pad)
    sees_a_block = q_pos(1) >= CMP_BLOCK - 1
    p_c = []
    p_sum = [None] * N_KV
    for u in range(N_UNITS):
        kh = unit_kv[u]
        s = jnp.where(cmp_mask, scores_c[u], NEG)
        m = jnp.max(s, axis=0, keepdims=True)
        p = jnp.exp2(s - m)
        l = jnp.sum(p, axis=0, keepdims=True)
        p = p * jnp.where(sees_a_block, 1.0 / l, 0.0)
        p_c.append(p.astype(BF16))
        for hh in range(HEADS_PER_UNIT):
            ph = p[:, hh * Q_BLOCK:(hh + 1) * Q_BLOCK]
            p_sum[kh] = ph if p_sum[kh] is None else p_sum[kh] + ph

    diff = q_pos(WIN_SPAN) - (w0 + lax.broadcasted_iota(jnp.int32, (WIN_SPAN, unit_w), 0))
    win_mask = (diff >= 0) & (diff < WINDOW)
    p_w = [_softmax_cols(scores_w[u], win_mask) for u in range(N_UNITS)]

    key_d = lax.broadcasted_iota(jnp.int32, (Q_BLOCK, unit_w), 0)
    causal_d = (t0 + key_d) <= q_pos(Q_BLOCK)
    m_d, p_d = [], []
    for u in range(N_UNITS):
        s = jnp.where(causal_d, scores_d[u], NEG)
        m_d.append(jnp.max(s, axis=0, keepdims=True))
        p_d.append(jnp.exp2(s - m_d[u]).astype(BF16))

    vc_t = jnp.concatenate([vc_t_ref[i] for i in range(n_cmp_pad // LANES)], axis=1)
    o_c = [_dot(vc_t[unit_kv[u] * HEAD_DIM:(unit_kv[u] + 1) * HEAD_DIM], p_c[u])
           for u in range(N_UNITS)]
    r_w = value_products(vw_t_ref, w0 // LANES, WIN_SPAN // LANES, lambda u: p_w[u])
    acc_d = value_products(vs_t_ref, c, 1, lambda u: p_d[u])

    ratio = SEL_BLOCK // CMP_STRIDE
    span = CMP_BLOCK // CMP_STRIDE
    j_idx = lax.broadcasted_iota(jnp.int32, (SUBLANES, Q_BLOCK), 0)
    cur = (t0 + lax.broadcasted_iota(jnp.int32, (SUBLANES, Q_BLOCK), 1)) // SEL_BLOCK
    first_own_block = t0 // SEL_BLOCK
    n_grp = S_BLOCKS // SUBLANES
    q_aug = [None] * N_UNITS
    for kh in range(N_KV):
        psum_ref[kh, :SUBLANES, :] = jnp.zeros((SUBLANES, Q_BLOCK), F32)
        psum_ref[kh, SUBLANES:, :] = p_sum[kh]
        imp_t = psum_ref[kh, pl.ds(SUBLANES - (span - 1), S_BLOCKS, stride=ratio), :]
        for k in range(2 - span, ratio):
            imp_t = imp_t + psum_ref[kh, pl.ds(SUBLANES + k, S_BLOCKS, stride=ratio), :]
        score = []
        for r in range(n_grp):
            j = j_idx + r * SUBLANES
            forced = (j == 0) | (j == cur) | (j == cur - 1)
            sc = jnp.where(j <= cur, imp_t[r * SUBLANES:(r + 1) * SUBLANES], -FORCE)
            score.append(jnp.where(forced, FORCE, sc))
        bias_rows = []
        for r in range(n_grp):
            cnt = jnp.zeros((SUBLANES, Q_BLOCK), jnp.int32)
            for jp in range(S_BLOCKS):
                rp = jp // SUBLANES
                other = score[rp][jp % SUBLANES:jp % SUBLANES + 1, :]
                ge = (other >= score[r]).astype(jnp.int32)
                gt = (other > score[r]).astype(jnp.int32)
                if rp < r:
                    ahead = ge
                elif rp > r:
                    ahead = gt
                else:
                    ahead = jnp.where(j_idx > (jp % SUBLANES), ge, gt)
                cnt = cnt + ahead
            past = (j_idx + r * SUBLANES) < first_own_block
            bias_rows.append(jnp.where(past, jnp.where(cnt < N_SELECT, 0.0, NEG), NEG))
        zero_rows = [jnp.zeros((LANES - S_BLOCKS, Q_BLOCK), F32)]
        bias_t = jnp.concatenate(
            zero_rows + bias_rows if _spare_lane0(kh) else bias_rows + zero_rows, axis=0)
        bias_q = jnp.concatenate([bias_t.T.astype(BF16)] * HEADS_PER_UNIT, axis=0)
        own = _own_lanes(lax.broadcasted_iota(jnp.int32, bias_q.shape, 1), kh)
        for u in range(N_UNITS):
            if unit_kv[u] == kh:
                q_aug[u] = jnp.where(own, q_unit[u], bias_q)

    o_w = [_normalize(r) for r in r_w]

    chunks = SEL_TILE // LANES
    last_tile = seq // SEL_TILE - 1

    def score_tile(tile, s_ref):
        k0 = pl.multiple_of(tile * SEL_TILE, SEL_TILE)
        kt = [ks_aug_ref[kh, pl.ds(k0, SEL_TILE), :] for kh in range(N_KV)]
        for u in range(N_UNITS):
            s_ref[u] = _dot_nt(kt[unit_kv[u]], q_aug[u])

    def stage(tile, s_cur, s_next, p_cur, p_prev, states, prefetch=True):
        pv_prev = value_products(vs_t_ref, jnp.maximum(tile - 1, 0) * chunks, chunks,
                                 lambda u: p_prev[u])
        if prefetch:
            score_tile(jnp.minimum(tile + 1, last_tile), s_next)
        out = []
        for u in range(N_UNITS):
            m, alpha_prev, acc = states[u]
            s = s_cur[u]
            m_new = jnp.maximum(m, jnp.max(s, axis=0, keepdims=True))
            p_cur[u] = jnp.exp2(s - m_new).astype(BF16)
            out.append((m_new, jnp.exp2(m - m_new), alpha_prev * acc + pv_prev[u]))
        return tuple(out)

    def trip(first_tile, states, prefetch_after=True):
        states = stage(first_tile, s_a_ref, s_b_ref, p_a_ref, p_b_ref, states)
        return stage(first_tile + 1, s_b_ref, s_a_ref, p_b_ref, p_a_ref, states, prefetch_after)

    blocks_per_trip = 2 * SEL_TILE // Q_BLOCK
    loop_trips = jnp.maximum((c + blocks_per_trip - 1) // blocks_per_trip - 1, 0)
    score_tile(0, s_a_ref)
    p_b_ref[...] = jnp.zeros(p_b_ref.shape, BF16)
    init = tuple((m_d[u], jnp.ones((1, unit_w), F32), acc_d[u]) for u in range(N_UNITS))
    sel_states = lax.fori_loop(0, loop_trips, lambda j, st: trip(2 * j, st), init)
    sel_states = trip(2 * loop_trips, sel_states, prefetch_after=False)
    pv_last = value_products(vs_t_ref, (2 * loop_trips + 1) * chunks, chunks,
                             lambda u: p_b_ref[u])
    o_s = [_normalize(alpha * acc + pv_last[u]) for u, (_, alpha, acc) in enumerate(sel_states)]

    mixed = []
    for hh in range(H_B):
        u, sl = hh // HEADS_PER_UNIT, slice((hh % HEADS_PER_UNIT) * Q_BLOCK,
                                            (hh % HEADS_PER_UNIT + 1) * Q_BLOCK)
        mixed.append(gates_t[hh:hh + 1] * o_c[u][:, sl]
                     + gates_t[H_B + hh:H_B + hh + 1] * o_s[u][:, sl]
                     + gates_t[2 * H_B + hh:2 * H_B + hh + 1] * o_w[u][:, sl])
    bmix = jnp.concatenate(mixed, axis=0).T
    o_ref[0] = (bmix * _rms_scale(bmix) * onw_ref[...]).astype(BF16)


def _nsa(q3, kcv, kv3, g3, gate_b, onw):
    b, s, _ = q3.shape
    n_cmp_pad = kcv.shape[2]
    unit_w = HEADS_PER_UNIT * Q_BLOCK
    const2 = lambda bi, ci: (0, 0)
    kv_spec = lambda col: pl.BlockSpec((1, s, D_KV), lambda bi, ci: (bi, 0, col))
    return pl.pallas_call(
        _nsa_kernel,
        grid=(b, s // Q_BLOCK),
        in_specs=[
            pl.BlockSpec((1, Q_BLOCK, D_B), lambda bi, ci: (bi, ci, 0)),
            pl.BlockSpec((None, 1, n_cmp_pad, D_KV), lambda bi, ci: (bi, 0, 0, 0)),
            pl.BlockSpec((None, 1, n_cmp_pad, D_KV), lambda bi, ci: (bi, 1, 0, 0)),
            kv_spec(0), kv_spec(1), kv_spec(2), kv_spec(3),
            pl.BlockSpec((1, Q_BLOCK, LANES), lambda bi, ci: (bi, ci, 0)),
            pl.BlockSpec((1, LANES), const2),
            pl.BlockSpec((1, D_B), const2),
        ],
        out_specs=pl.BlockSpec((1, Q_BLOCK, D_B), lambda bi, ci: (bi, ci, 0)),
        out_shape=jax.ShapeDtypeStruct((b, s, D_B), BF16),
        scratch_shapes=[
            pltpu.VMEM((N_KV, s, LANES), BF16),
            pltpu.VMEM((s // LANES, N_KV, V_ROWS, LANES), BF16),
            pltpu.VMEM((s // LANES, N_KV, V_ROWS, LANES), BF16),
            pltpu.VMEM((n_cmp_pad // LANES, D_KV, LANES), BF16),
            pltpu.VMEM((N_KV, SUBLANES + n_cmp_pad, Q_BLOCK), F32),
            pltpu.VMEM((N_UNITS, SEL_TILE, unit_w), F32),
            pltpu.VMEM((N_UNITS, SEL_TILE, unit_w), F32),
            pltpu.VMEM((N_UNITS, SEL_TILE, unit_w), BF16),
            pltpu.VMEM((N_UNITS, SEL_TILE, unit_w), BF16),
        ],
        compiler_params=pltpu.CompilerParams(
            dimension_semantics=("arbitrary", "arbitrary"), vmem_limit_bytes=VMEM_LIMIT_BYTES),
        name="nsa",
    )(q3, kcv, kcv, kv3, kv3, kv3, kv3, g3, gate_b, onw)


def _mix_ffn_kernel(x_ref, a_ref, b_ref, nw_ref, fnw_ref, woa_hbm, wob_hbm, wg_hbm, wu_hbm,
                    wd_hbm, o_ref, woa_ref, wob_ref, wg_ref, wu_ref, wd_ref, sem, *, final_norm):
    @pl.when(pl.program_id(0) == 0)
    def _load_weights():
        pairs = ((woa_hbm, woa_ref), (wob_hbm, wob_ref), (wg_hbm, wg_ref), (wu_hbm, wu_ref),
                 (wd_hbm, wd_ref))
        copies = [pltpu.make_async_copy(src, dst, sem.at[i]) for i, (src, dst) in enumerate(pairs)]
        for cp in copies:
            cp.start()
        for cp in copies:
            cp.wait()

    x1 = x_ref[...] + _dot(a_ref[...], woa_ref[...]) + _dot(b_ref[...], wob_ref[...])
    h = (x1 * _rms_scale(x1) * nw_ref[...]).astype(BF16)
    gate = _dot(h, wg_ref[...])
    up = _dot(h, wu_ref[...])
    act = (gate * jax.nn.sigmoid(gate) * up).astype(BF16)
    acc = x1 + _dot(act, wd_ref[...])
    if final_norm:
        acc = acc * _rms_scale(acc) * fnw_ref[...]
    o_ref[...] = acc


def _mix_ffn(x2d, a2d, b2d, woa, wob, nw, wg, wu, wd, fnw, final_norm):
    t = x2d.shape[0]
    row = lambda i: (i, 0)
    const = lambda i: (0, 0)
    weights = (woa, wob, wg, wu, wd)
    in_hbm = pl.BlockSpec(memory_space=pl.ANY)
    return pl.pallas_call(
        functools.partial(_mix_ffn_kernel, final_norm=final_norm),
        grid=(t // TM_FFN,),
        in_specs=[
            pl.BlockSpec((TM_FFN, D_MODEL), row),
            pl.BlockSpec((TM_FFN, D_A), row),
            pl.BlockSpec((TM_FFN, D_B), row),
            pl.BlockSpec((1, D_MODEL), const),
            pl.BlockSpec((1, D_MODEL), const),
        ] + [in_hbm] * len(weights),
        out_specs=pl.BlockSpec((TM_FFN, D_MODEL), row),
        out_shape=jax.ShapeDtypeStruct((t, D_MODEL), F32),
        scratch_shapes=[pltpu.VMEM(wt.shape, wt.dtype) for wt in weights]
        + [pltpu.SemaphoreType.DMA((len(weights),))],
        compiler_params=pltpu.CompilerParams(
            dimension_semantics=("arbitrary",), vmem_limit_bytes=VMEM_LIMIT_BYTES),
        name="mix_ffn",
    )(x2d, a2d, b2d, nw, fnw, *weights)


def kernel(x, norm_mix_w, w_in, gmlp_norm_w, gmlp_ws, gmlp_bs, cmp_pos_k, cmp_pos_v, cmp_k_w1,
           cmp_k_w2, cmp_v_w1, cmp_v_w2, gate_b, out_norm_a_w, out_norm_b_w, w_o, norm_ffn_w,
           w_gate, w_up, w_down, final_norm_w):
    b, s, d = x.shape
    assert d == D_MODEL and s // SEL_BLOCK == S_BLOCKS and s % Q_BLOCK == 0
    assert (b * s) % TM_PROJ == 0 and (b * s) % TM_FFN == 0 and s % (GMLP_CHUNKS * CHUNK) == 0
    t = b * s
    o_uv, o_q = 2 * D_A, 2 * D_A + D_B
    o_c, o_kv = o_q + 2 * D_KV, o_q + 6 * D_KV

    x2d = x.reshape(t, d)
    for l in range(DEPTH):
        w = w_in[l]
        wg_pad = jnp.pad(w[:, o_kv:], ((0, 0), (0, LANES - N_GATES * H_B)))
        uv, q, kc_raw, kv, g_raw = _in_proj(
            x2d, norm_mix_w[l][None, :], w[:, :o_uv].astype(BF16), w[:, o_uv:o_q].astype(BF16),
            w[:, o_q:o_c].astype(BF16), w[:, o_c:o_kv].astype(BF16), wg_pad.astype(BF16))

        bst = jnp.pad(gmlp_bs[l].T, ((0, 0), (0, LANES - H_A)))
        a_n = _gmlp(uv.reshape(b, s, 2 * D_A), gmlp_norm_w[l][None, :], gmlp_ws[l], bst,
                    out_norm_a_w[l][None, :])

        kv3 = kv.reshape(b, s, 4 * D_KV)
        pos = jnp.tile(jnp.stack([cmp_pos_k[l], cmp_pos_v[l]]), (1, 1, N_KV))
        w1 = jnp.stack([cmp_k_w1[l], cmp_v_w1[l]]).reshape(2, CMP_BLOCK, HEAD_DIM, CMP_HIDDEN)
        w1 = _per_kv_head_blockdiag(w1).astype(BF16)
        w2 = _per_kv_head_blockdiag(jnp.stack([cmp_k_w2[l], cmp_v_w2[l]])).astype(BF16)
        kcv = _compress(kc_raw.reshape(b, s, 2 * D_KV), pos, w1, w2)

        gb_pad = jnp.pad(gate_b[l], (0, LANES - N_GATES * H_B))[None, :]
        b_n = _nsa(q.reshape(b, s, D_B), kcv, kv3, g_raw.reshape(b, s, LANES), gb_pad,
                   out_norm_b_w[l][None, :])

        x2d = _mix_ffn(
            x2d, a_n.reshape(t, D_A), b_n.reshape(t, D_B),
            w_o[l][:D_A].astype(BF16), w_o[l][D_A:].astype(BF16), norm_ffn_w[l][None, :],
            w_gate[l].astype(BF16), w_up[l].astype(BF16), w_down[l].astype(BF16),
            final_norm_w[None, :], final_norm=(l == DEPTH - 1))
    return x2d.reshape(b, s, d)
```

```python
import functools

import jax
import jax.numpy as jnp
import numpy as np
from jax import lax
from jax.experimental import pallas as pl
from jax.experimental.pallas import tpu as pltpu

D_MODEL = 1024
DEPTH = 2
D_A = 512
D_B = 512
HEAD_DIM = 64
H_A = D_A // HEAD_DIM
H_B = D_B // HEAD_DIM
N_KV = 2
GQA = H_B // N_KV
D_KV = N_KV * HEAD_DIM
CHUNK = 128
CMP_BLOCK = 32
CMP_STRIDE = 16
CMP_HIDDEN = 256
SEL_BLOCK = 64
N_SELECT = 16
WINDOW = 512
Q_BLOCK = 128
N_GATES = 3
D_FF = 2816
EPS = 1e-6
NEG = -1e30
FORCE = 1e4
Q_SCALE = HEAD_DIM ** -0.5 * float(np.log2(np.e))

LANES = 128
SUBLANES = 8
VMEM_LIMIT_BYTES = 56 * 1024 * 1024

TM_PROJ = 1024
TM_FFN = 512
GMLP_CHUNKS = 8
SEL_TILE = 256
ONES_ROWS = 16
V_ROWS = HEAD_DIM + ONES_ROWS
HEADS_PER_UNIT = 2
N_UNITS = H_B // HEADS_PER_UNIT
WIN_SPAN = WINDOW + Q_BLOCK

F32 = jnp.float32
BF16 = jnp.bfloat16


def _gelu_tanh(x):
    c = np.float32(np.sqrt(2.0 / np.pi))
    return x * (0.5 * (1.0 + jnp.tanh(c * (x + 0.044715 * (x * x * x)))))


def _rms_scale(x):
    return lax.rsqrt(jnp.mean(x * x, axis=-1, keepdims=True) + EPS)


def _dot(a, b):
    return jnp.dot(a, b, preferred_element_type=F32)


def _dot_nt(a, b):
    return lax.dot_general(a, b, (((1,), (1,)), ((), ())), preferred_element_type=F32)


def _in_proj_kernel(x_ref, nw_ref, wuv_ref, wq_ref, wc_ref, wkv_ref, wg_ref,
                    uv_ref, q_ref, c_ref, kv_ref, g_ref):
    x = x_ref[...]
    h = (x * _rms_scale(x) * nw_ref[...]).astype(BF16)
    uv_ref[...] = _dot(h, wuv_ref[...])
    q_ref[...] = (_dot(h, wq_ref[...]) * Q_SCALE).astype(BF16)
    c_ref[...] = _dot(h, wc_ref[...])
    kv_ref[...] = _dot(h, wkv_ref[...]).astype(BF16)
    g_ref[...] = _dot(h, wg_ref[...])


def _in_proj(x2d, nw, wuv, wq, wc, wkv, wg):
    t = x2d.shape[0]
    const = lambda i: (0, 0)
    row = lambda i: (i, 0)
    return pl.pallas_call(
        _in_proj_kernel,
        grid=(t // TM_PROJ,),
        in_specs=[
            pl.BlockSpec((TM_PROJ, D_MODEL), row),
            pl.BlockSpec((1, D_MODEL), const),
            pl.BlockSpec((D_MODEL, 2 * D_A), const),
            pl.BlockSpec((D_MODEL, D_B), const),
            pl.BlockSpec((D_MODEL, 2 * D_KV), const),
            pl.BlockSpec((D_MODEL, 4 * D_KV), const),
            pl.BlockSpec((D_MODEL, LANES), const),
        ],
        out_specs=[
            pl.BlockSpec((TM_PROJ, 2 * D_A), row),
            pl.BlockSpec((TM_PROJ, D_B), row),
            pl.BlockSpec((TM_PROJ, 2 * D_KV), row),
            pl.BlockSpec((TM_PROJ, 4 * D_KV), row),
            pl.BlockSpec((TM_PROJ, LANES), row),
        ],
        out_shape=[
            jax.ShapeDtypeStruct((t, 2 * D_A), F32),
            jax.ShapeDtypeStruct((t, D_B), BF16),
            jax.ShapeDtypeStruct((t, 2 * D_KV), F32),
            jax.ShapeDtypeStruct((t, 4 * D_KV), BF16),
            jax.ShapeDtypeStruct((t, LANES), F32),
        ],
        compiler_params=pltpu.CompilerParams(
            dimension_semantics=("parallel",), vmem_limit_bytes=VMEM_LIMIT_BYTES),
        name="in_proj",
    )(x2d, nw, wuv, wq, wc, wkv, wg)


def _gmlp_kernel(u_ref, v_ref, nw_ref, ws_ref, bst_ref, onw_ref, o_ref):
    row = lax.broadcasted_iota(jnp.int32, (CHUNK, CHUNK), 0)
    col = lax.broadcasted_iota(jnp.int32, (CHUNK, CHUNK), 1)
    causal = col <= row
    low_half = col < HEAD_DIM
    bst = bst_ref[...]
    w = [jnp.where(causal, ws_ref[hh], 0.0).astype(BF16) for hh in range(H_A)]
    for ci in range(GMLP_CHUNKS):
        rows = slice(ci * CHUNK, (ci + 1) * CHUNK)
        u = _gelu_tanh(u_ref[0, rows, :])
        v = _gelu_tanh(v_ref[0, rows, :])
        vn = (v * _rms_scale(v) * nw_ref[...]).astype(BF16)
        tiles = []
        for i in range(D_A // LANES):
            vt = vn[:, i * LANES:(i + 1) * LANES]
            pair = [_dot(w[hh], vt) + bst[:, hh:hh + 1] for hh in (2 * i, 2 * i + 1)]
            tiles.append(jnp.where(low_half, pair[0], pair[1]))
        a = u * jnp.concatenate(tiles, axis=1)
        o_ref[0, rows, :] = (a * _rms_scale(a) * onw_ref[...]).astype(BF16)


def _gmlp(uv3, nw, ws, bst, onw):
    b, s, _ = uv3.shape
    rows = GMLP_CHUNKS * CHUNK
    const2 = lambda bi, ci: (0, 0)
    return pl.pallas_call(
        _gmlp_kernel,
        grid=(b, s // rows),
        in_specs=[
            pl.BlockSpec((1, rows, D_A), lambda bi, ci: (bi, ci, 0)),
            pl.BlockSpec((1, rows, D_A), lambda bi, ci: (bi, ci, 1)),
            pl.BlockSpec((1, D_A), const2),
            pl.BlockSpec((H_A, CHUNK, CHUNK), lambda bi, ci: (0, 0, 0)),
            pl.BlockSpec((CHUNK, LANES), const2),
            pl.BlockSpec((1, D_A), const2),
        ],
        out_specs=pl.BlockSpec((1, rows, D_A), lambda bi, ci: (bi, ci, 0)),
        out_shape=jax.ShapeDtypeStruct((b, s, D_A), BF16),
        compiler_params=pltpu.CompilerParams(
            dimension_semantics=("parallel", "parallel"), vmem_limit_bytes=VMEM_LIMIT_BYTES),
        name="gmlp",
    )(uv3, uv3, nw, ws, bst, onw)


def _compress_kernel(x_ref, pos_ref, w1_ref, w2_ref, o_ref, h1_ref, h2_ref):
    n_grp = o_ref.shape[2]
    for r in range(CMP_STRIDE):
        xr = x_ref[0, pl.ds(r, n_grp, stride=CMP_STRIDE), :]
        lo = _dot((xr + pos_ref[0, r:r + 1, :]).astype(BF16), w1_ref[0, r])
        hi = _dot((xr + pos_ref[0, CMP_STRIDE + r:CMP_STRIDE + r + 1, :]).astype(BF16),
                  w1_ref[0, CMP_STRIDE + r])
        if r == 0:
            h1_ref[...] = lo
            h2_ref[...] = hi
        else:
            h1_ref[...] += lo
            h2_ref[...] += hi
    h2_next = pltpu.roll(h2_ref[...], n_grp - 1, 0)
    hid = _gelu_tanh(h1_ref[...] + h2_next).astype(BF16)
    o_ref[0, 0] = _dot(hid, w2_ref[0]).astype(BF16)


def _compress(x3, pos, w1, w2):
    b, s, _ = x3.shape
    n_grp = s // CMP_STRIDE
    return pl.pallas_call(
        _compress_kernel,
        grid=(b, 2),
        in_specs=[
            pl.BlockSpec((1, s, D_KV), lambda bi, ti: (bi, 0, ti)),
            pl.BlockSpec((1,) + pos.shape[1:], lambda bi, ti: (ti, 0, 0)),
            pl.BlockSpec((1,) + w1.shape[1:], lambda bi, ti: (ti, 0, 0, 0)),
            pl.BlockSpec((1,) + w2.shape[1:], lambda bi, ti: (ti, 0, 0)),
        ],
        out_specs=pl.BlockSpec((1, 1, n_grp, D_KV), lambda bi, ti: (bi, ti, 0, 0)),
        out_shape=jax.ShapeDtypeStruct((b, 2, n_grp, D_KV), BF16),
        scratch_shapes=[pltpu.VMEM((n_grp, N_KV * CMP_HIDDEN), F32)] * 2,
        compiler_params=pltpu.CompilerParams(
            dimension_semantics=("parallel", "parallel"), vmem_limit_bytes=VMEM_LIMIT_BYTES),
        name="compress",
    )(x3, pos, w1, w2)


def _per_kv_head_blockdiag(w):
    k, n = w.shape[-2:]
    out = jnp.zeros(w.shape[:-2] + (N_KV * k, N_KV * n), w.dtype)
    for kh in range(N_KV):
        out = out.at[..., kh * k:(kh + 1) * k, kh * n:(kh + 1) * n].set(w)
    return out


S_BLOCKS = 64
assert N_KV == 2 and S_BLOCKS <= LANES - HEAD_DIM


def _softmax_cols(s, mask):
    s = jnp.where(mask, s, NEG)
    return jnp.exp2(s - jnp.max(s, axis=0, keepdims=True)).astype(BF16)


def _own_lanes(lane, kh):
    return (lane >= kh * HEAD_DIM) & (lane < (kh + 1) * HEAD_DIM)


def _spare_lane0(kh):
    return (1 - kh) * HEAD_DIM


def _normalize(r):
    return r[:HEAD_DIM] * (1.0 / r[HEAD_DIM:HEAD_DIM + 1])


def _nsa_kernel(q_ref, kc_ref, vc_ref, ks_ref, vs_ref, kw_ref, vw_ref, g_ref, gb_ref,
                onw_ref, o_ref, ks_aug_ref, vs_t_ref, vw_t_ref, vc_t_ref, psum_ref,
                s_a_ref, s_b_ref, p_a_ref, p_b_ref):
    c = pl.program_id(1)
    t0 = c * Q_BLOCK
    n_cmp_pad = kc_ref.shape[1]
    seq = ks_ref.shape[1]
    unit_w = HEADS_PER_UNIT * Q_BLOCK

    @pl.when(c == 0)
    def _prepare_batch_row():
        lane = lax.broadcasted_iota(jnp.int32, (LANES, LANES), 1)
        blk_of_row = lax.broadcasted_iota(jnp.int32, (LANES, LANES), 0) // SEL_BLOCK

        def prep(i, carry):
            r0 = pl.multiple_of(i * LANES, LANES)
            keys = ks_ref[0, pl.ds(r0, LANES), :]
            key_blk = blk_of_row + i * (LANES // SEL_BLOCK)
            for kh in range(N_KV):
                one_hot = jnp.where(lane - _spare_lane0(kh) == key_blk, 1.0, 0.0).astype(BF16)
                ks_aug_ref[kh, pl.ds(r0, LANES), :] = jnp.where(_own_lanes(lane, kh), keys, one_hot)
            for src, dst in ((vs_ref, vs_t_ref), (vw_ref, vw_t_ref)):
                chunk_t = src[0, pl.ds(r0, LANES), :].astype(F32).T.astype(BF16)
                for kh in range(N_KV):
                    dst[i, kh, :HEAD_DIM] = chunk_t[kh * HEAD_DIM:(kh + 1) * HEAD_DIM]
                    dst[i, kh, HEAD_DIM:] = jnp.ones((ONES_ROWS, LANES), BF16)
            return carry
        lax.fori_loop(0, seq // LANES, prep, 0)
        for i in range(n_cmp_pad // LANES):
            vc_t_ref[i] = vc_ref[0, i * LANES:(i + 1) * LANES, :].astype(F32).T.astype(BF16)

    q_lane = lax.broadcasted_iota(jnp.int32, (Q_BLOCK, LANES), 1)

    def placed(hh):
        tile = q_ref[0, :, (hh // 2) * LANES:(hh // 2 + 1) * LANES].astype(F32)
        kh = hh // GQA
        if hh % 2 != kh:
            tile = pltpu.roll(tile, HEAD_DIM, 1)
        return jnp.where(_own_lanes(q_lane, kh), tile, 0.0).astype(BF16)

    q_unit = [jnp.concatenate([placed(hh)
                               for hh in range(u * HEADS_PER_UNIT, (u + 1) * HEADS_PER_UNIT)],
                              axis=0) for u in range(N_UNITS)]
    unit_kv = [u * HEADS_PER_UNIT // GQA for u in range(N_UNITS)]
    gates_t = jax.nn.sigmoid(g_ref[0] + gb_ref[...]).T

    def q_pos(rows):
        return t0 + (lax.broadcasted_iota(jnp.int32, (rows, unit_w), 1) & (Q_BLOCK - 1))

    def value_products(v_ref, chunk0, n_chunks, p_of_unit):
        v_aug = [jnp.concatenate([v_ref[chunk0 + j, kh] for j in range(n_chunks)], axis=1)
                 if n_chunks > 1 else v_ref[chunk0, kh] for kh in range(N_KV)]
        return [_dot(v_aug[unit_kv[u]], p_of_unit(u)) for u in range(N_UNITS)]

    w0 = pl.multiple_of(jnp.maximum(t0 - WINDOW, 0), Q_BLOCK)
    kc = kc_ref[0]
    kw = kw_ref[0, pl.ds(w0, WIN_SPAN), :]
    kd = ks_ref[0, pl.ds(pl.multiple_of(t0, Q_BLOCK), Q_BLOCK), :]
    scores_c = [_dot_nt(kc, q_unit[u]) for u in range(N_UNITS)]
    scores_w = [_dot_nt(kw, q_unit[u]) for u in range(N_UNITS)]
    scores_d = [_dot_nt(kd, q_unit[u]) for u in range(N_UNITS)]

    n_idx = lax.broadcasted_iota(jnp.int32, (n_cmp_pad, unit_w), 0)
    cmp_mask = (n_idx * CMP_STRIDE + (CMP_BLOCK - 1)) <= q_pos(n_cmp_pad)
    sees_a_block = q_pos(1) >= CMP_BLOCK - 1
    p_c = []
    p_sum = [None] * N_KV
    for u in range(N_UNITS):
        kh = unit_kv[u]
        s = jnp.where(cmp_mask, scores_c[u], NEG)
        m = jnp.max(s, axis=0, keepdims=True)
        p = jnp.exp2(s - m)
        l = jnp.sum(p, axis=0, keepdims=True)
        p = p * jnp.where(sees_a_block, 1.0 / l, 0.0)
        p_c.append(p.astype(BF16))
        for hh in range(HEADS_PER_UNIT):
            ph = p[:, hh * Q_BLOCK:(hh + 1) * Q_BLOCK]
            p_sum[kh] = ph if p_sum[kh] is None else p_sum[kh] + ph

    diff = q_pos(WIN_SPAN) - (w0 + lax.broadcasted_iota(jnp.int32, (WIN_SPAN, unit_w), 0))
    win_mask = (diff >= 0) & (diff < WINDOW)
    p_w = [_softmax_cols(scores_w[u], win_mask) for u in range(N_UNITS)]

    key_d = lax.broadcasted_iota(jnp.int32, (Q_BLOCK, unit_w), 0)
    causal_d = (t0 + key_d) <= q_pos(Q_BLOCK)
    m_d, p_d = [], []
    for u in range(N_UNITS):
        s = jnp.where(causal_d, scores_d[u], NEG)
        m_d.append(jnp.max(s, axis=0, keepdims=True))
        p_d.append(jnp.exp2(s - m_d[u]).astype(BF16))

    vc_t = jnp.concatenate([vc_t_ref[i] for i in range(n_cmp_pad // LANES)], axis=1)
    o_c = [_dot(vc_t[unit_kv[u] * HEAD_DIM:(unit_kv[u] + 1) * HEAD_DIM], p_c[u])
           for u in range(N_UNITS)]
    r_w = value_products(vw_t_ref, w0 // LANES, WIN_SPAN // LANES, lambda u: p_w[u])
    acc_d = value_products(vs_t_ref, c, 1, lambda u: p_d[u])

    ratio = SEL_BLOCK // CMP_STRIDE
    span = CMP_BLOCK // CMP_STRIDE
    j_idx = lax.broadcasted_iota(jnp.int32, (SUBLANES, Q_BLOCK), 0)
    cur = (t0 + lax.broadcasted_iota(jnp.int32, (SUBLANES, Q_BLOCK), 1)) // SEL_BLOCK
    first_own_block = t0 // SEL_BLOCK
    n_grp = S_BLOCKS // SUBLANES
    q_aug = [None] * N_UNITS
    for kh in range(N_KV):
        psum_ref[kh, :SUBLANES, :] = jnp.zeros((SUBLANES, Q_BLOCK), F32)
        psum_ref[kh, SUBLANES:, :] = p_sum[kh]
        imp_t = psum_ref[kh, pl.ds(SUBLANES - (span - 1), S_BLOCKS, stride=ratio), :]
        for k in range(2 - span, ratio):
            imp_t = imp_t + psum_ref[kh, pl.ds(SUBLANES + k, S_BLOCKS, stride=ratio), :]
        score = []
        for r in range(n_grp):
            j = j_idx + r * SUBLANES
            forced = (j == 0) | (j == cur) | (j == cur - 1)
            sc = jnp.where(j <= cur, imp_t[r * SUBLANES:(r + 1) * SUBLANES], -FORCE)
            score.append(jnp.where(forced, FORCE, sc))
        bias_rows = []
        for r in range(n_grp):
            cnt = jnp.zeros((SUBLANES, Q_BLOCK), jnp.int32)
            for jp in range(S_BLOCKS):
                rp = jp // SUBLANES
                other = score[rp][jp % SUBLANES:jp % SUBLANES + 1, :]
                ge = (other >= score[r]).astype(jnp.int32)
                gt = (other > score[r]).astype(jnp.int32)
                if rp < r:
                    ahead = ge
                elif rp > r:
                    ahead = gt
                else:
                    ahead = jnp.where(j_idx > (jp % SUBLANES), ge, gt)
                cnt = cnt + ahead
            past = (j_idx + r * SUBLANES) < first_own_block
            bias_rows.append(jnp.where(past, jnp.where(cnt < N_SELECT, 0.0, NEG), NEG))
        zero_rows = [jnp.zeros((LANES - S_BLOCKS, Q_BLOCK), F32)]
        bias_t = jnp.concatenate(
            zero_rows + bias_rows if _spare_lane0(kh) else bias_rows + zero_rows, axis=0)
        bias_q = jnp.concatenate([bias_t.T.astype(BF16)] * HEADS_PER_UNIT, axis=0)
        own = _own_lanes(lax.broadcasted_iota(jnp.int32, bias_q.shape, 1), kh)
        for u in range(N_UNITS):
            if unit_kv[u] == kh:
                q_aug[u] = jnp.where(own, q_unit[u], bias_q)

    o_w = [_normalize(r) for r in r_w]

    chunks = SEL_TILE // LANES
    last_tile = seq // SEL_TILE - 1

    def score_tile(tile, s_ref):
        k0 = pl.multiple_of(tile * SEL_TILE, SEL_TILE)
        kt = [ks_aug_ref[kh, pl.ds(k0, SEL_TILE), :] for kh in range(N_KV)]
        for u in range(N_UNITS):
            s_ref[u] = _dot_nt(kt[unit_kv[u]], q_aug[u])

    def stage(tile, s_cur, s_next, p_cur, p_prev, states, prefetch=True):
        pv_prev = value_products(vs_t_ref, jnp.maximum(tile - 1, 0) * chunks, chunks,
                                 lambda u: p_prev[u])
        if prefetch:
            score_tile(jnp.minimum(tile + 1, last_tile), s_next)
        out = []
        for u in range(N_UNITS):
            m, alpha_prev, acc = states[u]
            s = s_cur[u]
            m_new = jnp.maximum(m, jnp.max(s, axis=0, keepdims=True))
            p_cur[u] = jnp.exp2(s - m_new).astype(BF16)
            out.append((m_new, jnp.exp2(m - m_new), alpha_prev * acc + pv_prev[u]))
        return tuple(out)

    def trip(first_tile, states, prefetch_after=True):
        states = stage(first_tile, s_a_ref, s_b_ref, p_a_ref, p_b_ref, states)
        return stage(first_tile + 1, s_b_ref, s_a_ref, p_b_ref, p_a_ref, states, prefetch_after)

    blocks_per_trip = 2 * SEL_TILE // Q_BLOCK
    loop_trips = jnp.maximum((c + blocks_per_trip - 1) // blocks_per_trip - 2, 0)
    score_tile(0, s_a_ref)
    p_b_ref[...] = jnp.zeros(p_b_ref.shape, BF16)
    init = tuple((m_d[u], jnp.ones((1, unit_w), F32), acc_d[u]) for u in range(N_UNITS))
    sel_states = trip(0, init)
    sel_states = lax.fori_loop(0, loop_trips, lambda j, st: trip(2 * (j + 1), st), sel_states)
    sel_states = trip(2 * (loop_trips + 1), sel_states, prefetch_after=False)
    pv_last = value_products(vs_t_ref, (2 * loop_trips + 3) * chunks, chunks,
                             lambda u: p_b_ref[u])
    o_s = [_normalize(alpha * acc + pv_last[u]) for u, (_, alpha, acc) in enumerate(sel_states)]

    mixed = []
    for hh in range(H_B):
        u, sl = hh // HEADS_PER_UNIT, slice((hh % HEADS_PER_UNIT) * Q_BLOCK,
                                            (hh % HEADS_PER_UNIT + 1) * Q_BLOCK)
        mixed.append(gates_t[hh:hh + 1] * o_c[u][:, sl]
                     + gates_t[H_B + hh:H_B + hh + 1] * o_s[u][:, sl]
                     + gates_t[2 * H_B + hh:2 * H_B + hh + 1] * o_w[u][:, sl])
    bmix = jnp.concatenate(mixed, axis=0).T
    o_ref[0] = (bmix * _rms_scale(bmix) * onw_ref[...]).astype(BF16)


def _nsa(q3, kcv, kv3, g3, gate_b, onw):
    b, s, _ = q3.shape
    n_cmp_pad = kcv.shape[2]
    unit_w = HEADS_PER_UNIT * Q_BLOCK
    const2 = lambda bi, ci: (0, 0)
    kv_spec = lambda col: pl.BlockSpec((1, s, D_KV), lambda bi, ci: (bi, 0, col))
    return pl.pallas_call(
        _nsa_kernel,
        grid=(b, s // Q_BLOCK),
        in_specs=[
            pl.BlockSpec((1, Q_BLOCK, D_B), lambda bi, ci: (bi, ci, 0)),
            pl.BlockSpec((None, 1, n_cmp_pad, D_KV), lambda bi, ci: (bi, 0, 0, 0)),
            pl.BlockSpec((None, 1, n_cmp_pad, D_KV), lambda bi, ci: (bi, 1, 0, 0)),
            kv_spec(0), kv_spec(1), kv_spec(2), kv_spec(3),
            pl.BlockSpec((1, Q_BLOCK, LANES), lambda bi, ci: (bi, ci, 0)),
            pl.BlockSpec((1, LANES), const2),
            pl.BlockSpec((1, D_B), const2),
        ],
        out_specs=pl.BlockSpec((1, Q_BLOCK, D_B), lambda bi, ci: (bi, ci, 0)),
        out_shape=jax.ShapeDtypeStruct((b, s, D_B), BF16),
        scratch_shapes=[
            pltpu.VMEM((N_KV, s, LANES), BF16),
            pltpu.VMEM((s // LANES, N_KV, V_ROWS, LANES), BF16),
            pltpu.VMEM((s // LANES, N_KV, V_ROWS, LANES), BF16),
            pltpu.VMEM((n_cmp_pad // LANES, D_KV, LANES), BF16),
            pltpu.VMEM((N_KV, SUBLANES + n_cmp_pad, Q_BLOCK), F32),
            pltpu.VMEM((N_UNITS, SEL_TILE, unit_w), F32),
            pltpu.VMEM((N_UNITS, SEL_TILE, unit_w), F32),
            pltpu.VMEM((N_UNITS, SEL_TILE, unit_w), BF16),
            pltpu.VMEM((N_UNITS, SEL_TILE, unit_w), BF16),
        ],
        compiler_params=pltpu.CompilerParams(
            dimension_semantics=("arbitrary", "arbitrary"), vmem_limit_bytes=VMEM_LIMIT_BYTES),
        name="nsa",
    )(q3, kcv, kcv, kv3, kv3, kv3, kv3, g3, gate_b, onw)


def _mix_ffn_kernel(x_ref, a_ref, b_ref, nw_ref, fnw_ref, woa_hbm, wob_hbm, wg_hbm, wu_hbm,
                    wd_hbm, o_ref, woa_ref, wob_ref, wg_ref, wu_ref, wd_ref, sem, *, final_norm):
    @pl.when(pl.program_id(0) == 0)
    def _load_weights():
        pairs = ((woa_hbm, woa_ref), (wob_hbm, wob_ref), (wg_hbm, wg_ref), (wu_hbm, wu_ref),
                 (wd_hbm, wd_ref))
        copies = [pltpu.make_async_copy(src, dst, sem.at[i]) for i, (src, dst) in enumerate(pairs)]
        for cp in copies:
            cp.start()
        for cp in copies:
            cp.wait()

    x1 = x_ref[...] + _dot(a_ref[...], woa_ref[...]) + _dot(b_ref[...], wob_ref[...])
    h = (x1 * _rms_scale(x1) * nw_ref[...]).astype(BF16)
    gate = _dot(h, wg_ref[...])
    up = _dot(h, wu_ref[...])
    act = (gate * jax.nn.sigmoid(gate) * up).astype(BF16)
    acc = x1 + _dot(act, wd_ref[...])
    if final_norm:
        acc = acc * _rms_scale(acc) * fnw_ref[...]
    o_ref[...] = acc


def _mix_ffn(x2d, a2d, b2d, woa, wob, nw, wg, wu, wd, fnw, final_norm):
    t = x2d.shape[0]
    row = lambda i: (i, 0)
    const = lambda i: (0, 0)
    weights = (woa, wob, wg, wu, wd)
    in_hbm = pl.BlockSpec(memory_space=pl.ANY)
    return pl.pallas_call(
        functools.partial(_mix_ffn_kernel, final_norm=final_norm),
        grid=(t // TM_FFN,),
        in_specs=[
            pl.BlockSpec((TM_FFN, D_MODEL), row),
            pl.BlockSpec((TM_FFN, D_A), row),
            pl.BlockSpec((TM_FFN, D_B), row),
            pl.BlockSpec((1, D_MODEL), const),
            pl.BlockSpec((1, D_MODEL), const),
        ] + [in_hbm] * len(weights),
        out_specs=pl.BlockSpec((TM_FFN, D_MODEL), row),
        out_shape=jax.ShapeDtypeStruct((t, D_MODEL), F32),
        scratch_shapes=[pltpu.VMEM(wt.shape, wt.dtype) for wt in weights]
        + [pltpu.SemaphoreType.DMA((len(weights),))],
        compiler_params=pltpu.CompilerParams(
            dimension_semantics=("arbitrary",), vmem_limit_bytes=VMEM_LIMIT_BYTES),
        name="mix_ffn",
    )(x2d, a2d, b2d, nw, fnw, *weights)


def kernel(x, norm_mix_w, w_in, gmlp_norm_w, gmlp_ws, gmlp_bs, cmp_pos_k, cmp_pos_v, cmp_k_w1,
           cmp_k_w2, cmp_v_w1, cmp_v_w2, gate_b, out_norm_a_w, out_norm_b_w, w_o, norm_ffn_w,
           w_gate, w_up, w_down, final_norm_w):
    b, s, d = x.shape
    assert d == D_MODEL and s // SEL_BLOCK == S_BLOCKS and s % Q_BLOCK == 0
    assert (b * s) % TM_PROJ == 0 and (b * s) % TM_FFN == 0 and s % (GMLP_CHUNKS * CHUNK) == 0
    t = b * s
    o_uv, o_q = 2 * D_A, 2 * D_A + D_B
    o_c, o_kv = o_q + 2 * D_KV, o_q + 6 * D_KV

    x2d = x.reshape(t, d)
    for l in range(DEPTH):
        w = w_in[l]
        wg_pad = jnp.pad(w[:, o_kv:], ((0, 0), (0, LANES - N_GATES * H_B)))
        uv, q, kc_raw, kv, g_raw = _in_proj(
            x2d, norm_mix_w[l][None, :], w[:, :o_uv].astype(BF16), w[:, o_uv:o_q].astype(BF16),
            w[:, o_q:o_c].astype(BF16), w[:, o_c:o_kv].astype(BF16), wg_pad.astype(BF16))

        bst = jnp.pad(gmlp_bs[l].T, ((0, 0), (0, LANES - H_A)))
        a_n = _gmlp(uv.reshape(b, s, 2 * D_A), gmlp_norm_w[l][None, :], gmlp_ws[l], bst,
                    out_norm_a_w[l][None, :])

        kv3 = kv.reshape(b, s, 4 * D_KV)
        pos = jnp.tile(jnp.stack([cmp_pos_k[l], cmp_pos_v[l]]), (1, 1, N_KV))
        w1 = jnp.stack([cmp_k_w1[l], cmp_v_w1[l]]).reshape(2, CMP_BLOCK, HEAD_DIM, CMP_HIDDEN)
        w1 = _per_kv_head_blockdiag(w1).astype(BF16)
        w2 = _per_kv_head_blockdiag(jnp.stack([cmp_k_w2[l], cmp_v_w2[l]])).astype(BF16)
        kcv = _compress(kc_raw.reshape(b, s, 2 * D_KV), pos, w1, w2)

        gb_pad = jnp.pad(gate_b[l], (0, LANES - N_GATES * H_B))[None, :]
        b_n = _nsa(q.reshape(b, s, D_B), kcv, kv3, g_raw.reshape(b, s, LANES), gb_pad,
                   out_norm_b_w[l][None, :])

        x2d = _mix_ffn(
            x2d, a_n.reshape(t, D_A), b_n.reshape(t, D_B),
            w_o[l][:D_A].astype(BF16), w_o[l][D_A:].astype(BF16), norm_ffn_w[l][None, :],
            w_gate[l].astype(BF16), w_up[l].astype(BF16), w_down[l].astype(BF16),
            final_norm_w[None, :], final_norm=(l == DEPTH - 1))
    return x2d.reshape(b, s, d)
```

```python
import functools

import jax
import jax.numpy as jnp
import numpy as np
from jax import lax
from jax.experimental import pallas as pl
from jax.experimental.pallas import tpu as pltpu

D_MODEL = 1024
DEPTH = 2
D_A = 512
D_B = 512
HEAD_DIM = 64
H_A = D_A // HEAD_DIM
H_B = D_B // HEAD_DIM
N_KV = 2
GQA = H_B // N_KV
D_KV = N_KV * HEAD_DIM
CHUNK = 128
CMP_BLOCK = 32
CMP_STRIDE = 16
CMP_HIDDEN = 256
SEL_BLOCK = 64
N_SELECT = 16
WINDOW = 512
Q_BLOCK = 128
N_GATES = 3
D_FF = 2816
EPS = 1e-6
NEG = -1e30
FORCE = 1e4
Q_SCALE = HEAD_DIM ** -0.5 * float(np.log2(np.e))

LANES = 128
SUBLANES = 8
VMEM_LIMIT_BYTES = 56 * 1024 * 1024

TM_PROJ = 1024
TM_FFN = 512
GMLP_CHUNKS = 8
SEL_TILE = 256
ONES_ROWS = 16
V_ROWS = HEAD_DIM + ONES_ROWS
HEADS_PER_UNIT = 2
N_UNITS = H_B // HEADS_PER_UNIT
WIN_SPAN = WINDOW + Q_BLOCK

F32 = jnp.float32
BF16 = jnp.bfloat16


def _gelu_tanh(x):
    c = np.float32(np.sqrt(2.0 / np.pi))
    return x * (0.5 * (1.0 + jnp.tanh(c * (x + 0.044715 * (x * x * x)))))


def _rms_scale(x):
    return lax.rsqrt(jnp.mean(x * x, axis=-1, keepdims=True) + EPS)


def _dot(a, b):
    return jnp.dot(a, b, preferred_element_type=F32)


def _dot_nt(a, b):
    return lax.dot_general(a, b, (((1,), (1,)), ((), ())), preferred_element_type=F32)


def _in_proj_kernel(x_ref, nw_ref, wuv_ref, wq_ref, wc_ref, wkv_ref, wg_ref,
                    gnw_ref, ws_ref, bst_ref, onw_ref,
                    a_ref, q_ref, c_ref, kv_ref, g_ref, uv_ref):
    x = x_ref[...]
    h = (x * _rms_scale(x) * nw_ref[...]).astype(BF16)
    uv_ref[...] = _dot(h, wuv_ref[...])
    q_ref[...] = (_dot(h, wq_ref[...]) * Q_SCALE).astype(BF16)
    c_ref[...] = _dot(h, wc_ref[...])
    kv_ref[...] = _dot(h, wkv_ref[...]).astype(BF16)
    g_ref[...] = _dot(h, wg_ref[...])
    row = lax.broadcasted_iota(jnp.int32, (CHUNK, CHUNK), 0)
    col = lax.broadcasted_iota(jnp.int32, (CHUNK, CHUNK), 1)
    causal = col <= row
    low_half = col < HEAD_DIM
    bst = bst_ref[...]
    w = [jnp.where(causal, ws_ref[hh], 0.0).astype(BF16) for hh in range(H_A)]
    for ci in range(TM_PROJ // CHUNK):
        rows = slice(ci * CHUNK, (ci + 1) * CHUNK)
        u = _gelu_tanh(uv_ref[rows, :D_A])
        v = _gelu_tanh(uv_ref[rows, D_A:])
        vn = (v * _rms_scale(v) * gnw_ref[...]).astype(BF16)
        tiles = []
        for i in range(D_A // LANES):
            vt = vn[:, i * LANES:(i + 1) * LANES]
            pair = [_dot(w[hh], vt) + bst[:, hh:hh + 1] for hh in (2 * i, 2 * i + 1)]
            tiles.append(jnp.where(low_half, pair[0], pair[1]))
        a = u * jnp.concatenate(tiles, axis=1)
        a_ref[rows, :] = (a * _rms_scale(a) * onw_ref[...]).astype(BF16)


def _in_proj(x2d, nw, wuv, wq, wc, wkv, wg, gnw, ws, bst, onw):
    t = x2d.shape[0]
    const = lambda i: (0, 0)
    row = lambda i: (i, 0)
    return pl.pallas_call(
        _in_proj_kernel,
        grid=(t // TM_PROJ,),
        in_specs=[
            pl.BlockSpec((TM_PROJ, D_MODEL), row),
            pl.BlockSpec((1, D_MODEL), const),
            pl.BlockSpec((D_MODEL, 2 * D_A), const),
            pl.BlockSpec((D_MODEL, D_B), const),
            pl.BlockSpec((D_MODEL, 2 * D_KV), const),
            pl.BlockSpec((D_MODEL, 4 * D_KV), const),
            pl.BlockSpec((D_MODEL, LANES), const),
            pl.BlockSpec((1, D_A), const),
            pl.BlockSpec((H_A, CHUNK, CHUNK), lambda i: (0, 0, 0)),
            pl.BlockSpec((CHUNK, LANES), const),
            pl.BlockSpec((1, D_A), const),
        ],
        out_specs=[
            pl.BlockSpec((TM_PROJ, D_A), row),
            pl.BlockSpec((TM_PROJ, D_B), row),
            pl.BlockSpec((TM_PROJ, 2 * D_KV), row),
            pl.BlockSpec((TM_PROJ, 4 * D_KV), row),
            pl.BlockSpec((TM_PROJ, LANES), row),
        ],
        out_shape=[
            jax.ShapeDtypeStruct((t, D_A), BF16),
            jax.ShapeDtypeStruct((t, D_B), BF16),
            jax.ShapeDtypeStruct((t, 2 * D_KV), F32),
            jax.ShapeDtypeStruct((t, 4 * D_KV), BF16),
            jax.ShapeDtypeStruct((t, LANES), F32),
        ],
        compiler_params=pltpu.CompilerParams(
            dimension_semantics=("parallel",), vmem_limit_bytes=VMEM_LIMIT_BYTES),
        scratch_shapes=[pltpu.VMEM((TM_PROJ, 2 * D_A), F32)],
        name="in_proj",
    )(x2d, nw, wuv, wq, wc, wkv, wg, gnw, ws, bst, onw)


def _gmlp_kernel(u_ref, v_ref, nw_ref, ws_ref, bst_ref, onw_ref, o_ref):
    row = lax.broadcasted_iota(jnp.int32, (CHUNK, CHUNK), 0)
    col = lax.broadcasted_iota(jnp.int32, (CHUNK, CHUNK), 1)
    causal = col <= row
    low_half = col < HEAD_DIM
    bst = bst_ref[...]
    w = [jnp.where(causal, ws_ref[hh], 0.0).astype(BF16) for hh in range(H_A)]
    for ci in range(GMLP_CHUNKS):
        rows = slice(ci * CHUNK, (ci + 1) * CHUNK)
        u = _gelu_tanh(u_ref[0, rows, :])
        v = _gelu_tanh(v_ref[0, rows, :])
        vn = (v * _rms_scale(v) * nw_ref[...]).astype(BF16)
        tiles = []
        for i in range(D_A // LANES):
            vt = vn[:, i * LANES:(i + 1) * LANES]
            pair = [_dot(w[hh], vt) + bst[:, hh:hh + 1] for hh in (2 * i, 2 * i + 1)]
            tiles.append(jnp.where(low_half, pair[0], pair[1]))
        a = u * jnp.concatenate(tiles, axis=1)
        o_ref[0, rows, :] = (a * _rms_scale(a) * onw_ref[...]).astype(BF16)


def _gmlp(uv3, nw, ws, bst, onw):
    b, s, _ = uv3.shape
    rows = GMLP_CHUNKS * CHUNK
    const2 = lambda bi, ci: (0, 0)
    return pl.pallas_call(
        _gmlp_kernel,
        grid=(b, s // rows),
        in_specs=[
            pl.BlockSpec((1, rows, D_A), lambda bi, ci: (bi, ci, 0)),
            pl.BlockSpec((1, rows, D_A), lambda bi, ci: (bi, ci, 1)),
            pl.BlockSpec((1, D_A), const2),
            pl.BlockSpec((H_A, CHUNK, CHUNK), lambda bi, ci: (0, 0, 0)),
            pl.BlockSpec((CHUNK, LANES), const2),
            pl.BlockSpec((1, D_A), const2),
        ],
        out_specs=pl.BlockSpec((1, rows, D_A), lambda bi, ci: (bi, ci, 0)),
        out_shape=jax.ShapeDtypeStruct((b, s, D_A), BF16),
        compiler_params=pltpu.CompilerParams(
            dimension_semantics=("parallel", "parallel"), vmem_limit_bytes=VMEM_LIMIT_BYTES),
        name="gmlp",
    )(uv3, uv3, nw, ws, bst, onw)


def _compress_kernel(x_ref, pos_ref, w1_ref, w2_ref, o_ref, h1_ref, h2_ref):
    n_grp = o_ref.shape[2]
    for r in range(CMP_STRIDE):
        xr = x_ref[0, pl.ds(r, n_grp, stride=CMP_STRIDE), :]
        lo = _dot((xr + pos_ref[0, r:r + 1, :]).astype(BF16), w1_ref[0, r])
        hi = _dot((xr + pos_ref[0, CMP_STRIDE + r:CMP_STRIDE + r + 1, :]).astype(BF16),
                  w1_ref[0, CMP_STRIDE + r])
        if r == 0:
            h1_ref[...] = lo
            h2_ref[...] = hi
        else:
            h1_ref[...] += lo
            h2_ref[...] += hi
    h2_next = pltpu.roll(h2_ref[...], n_grp - 1, 0)
    hid = _gelu_tanh(h1_ref[...] + h2_next).astype(BF16)
    o_ref[0, 0] = _dot(hid, w2_ref[0]).astype(BF16)


def _compress(x3, pos, w1, w2):
    b, s, _ = x3.shape
    n_grp = s // CMP_STRIDE
    return pl.pallas_call(
        _compress_kernel,
        grid=(b, 2),
        in_specs=[
            pl.BlockSpec((1, s, D_KV), lambda bi, ti: (bi, 0, ti)),
            pl.BlockSpec((1,) + pos.shape[1:], lambda bi, ti: (ti, 0, 0)),
            pl.BlockSpec((1,) + w1.shape[1:], lambda bi, ti: (ti, 0, 0, 0)),
            pl.BlockSpec((1,) + w2.shape[1:], lambda bi, ti: (ti, 0, 0)),
        ],
        out_specs=pl.BlockSpec((1, 1, n_grp, D_KV), lambda bi, ti: (bi, ti, 0, 0)),
        out_shape=jax.ShapeDtypeStruct((b, 2, n_grp, D_KV), BF16),
        scratch_shapes=[pltpu.VMEM((n_grp, N_KV * CMP_HIDDEN), F32)] * 2,
        compiler_params=pltpu.CompilerParams(
            dimension_semantics=("parallel", "parallel"), vmem_limit_bytes=VMEM_LIMIT_BYTES),
        name="compress",
    )(x3, pos, w1, w2)


def _per_kv_head_blockdiag(w):
    k, n = w.shape[-2:]
    out = jnp.zeros(w.shape[:-2] + (N_KV * k, N_KV * n), w.dtype)
    for kh in range(N_KV):
        out = out.at[..., kh * k:(kh + 1) * k, kh * n:(kh + 1) * n].set(w)
    return out


S_BLOCKS = 64
assert N_KV == 2 and S_BLOCKS <= LANES - HEAD_DIM


def _softmax_cols(s, mask):
    s = jnp.where(mask, s, NEG)
    return jnp.exp2(s - jnp.max(s, axis=0, keepdims=True)).astype(BF16)


def _own_lanes(lane, kh):
    return (lane >= kh * HEAD_DIM) & (lane < (kh + 1) * HEAD_DIM)


def _spare_lane0(kh):
    return (1 - kh) * HEAD_DIM


def _normalize(r):
    return r[:HEAD_DIM] * (1.0 / r[HEAD_DIM:HEAD_DIM + 1])


def _nsa_kernel(q_ref, kc_ref, vc_ref, ks_ref, vs_ref, kw_ref, vw_ref, g_ref, gb_ref,
                onw_ref, o_ref, ks_aug_ref, vs_t_ref, vw_t_ref, vc_t_ref, psum_ref,
                s_a_ref, s_b_ref, p_a_ref, p_b_ref):
    c = pl.program_id(1)
    t0 = c * Q_BLOCK
    n_cmp_pad = kc_ref.shape[1]
    seq = ks_ref.shape[1]
    unit_w = HEADS_PER_UNIT * Q_BLOCK

    @pl.when(c == 0)
    def _prepare_batch_row():
        lane = lax.broadcasted_iota(jnp.int32, (LANES, LANES), 1)
        blk_of_row = lax.broadcasted_iota(jnp.int32, (LANES, LANES), 0) // SEL_BLOCK

        def prep(i, carry):
            r0 = pl.multiple_of(i * LANES, LANES)
            keys = ks_ref[0, pl.ds(r0, LANES), :]
            key_blk = blk_of_row + i * (LANES // SEL_BLOCK)
            for kh in range(N_KV):
                one_hot = jnp.where(lane - _spare_lane0(kh) == key_blk, 1.0, 0.0).astype(BF16)
                ks_aug_ref[kh, pl.ds(r0, LANES), :] = jnp.where(_own_lanes(lane, kh), keys, one_hot)
            for src, dst in ((vs_ref, vs_t_ref), (vw_ref, vw_t_ref)):
                chunk_t = src[0, pl.ds(r0, LANES), :].astype(F32).T.astype(BF16)
                for kh in range(N_KV):
                    dst[i, kh, :HEAD_DIM] = chunk_t[kh * HEAD_DIM:(kh + 1) * HEAD_DIM]
                    dst[i, kh, HEAD_DIM:] = jnp.ones((ONES_ROWS, LANES), BF16)
            return carry
        lax.fori_loop(0, seq // LANES, prep, 0)
        for i in range(n_cmp_pad // LANES):
            vc_t_ref[i] = vc_ref[0, i * LANES:(i + 1) * LANES, :].astype(F32).T.astype(BF16)

    q_lane = lax.broadcasted_iota(jnp.int32, (Q_BLOCK, LANES), 1)

    def placed(hh):
        tile = q_ref[0, :, (hh // 2) * LANES:(hh // 2 + 1) * LANES].astype(F32)
        kh = hh // GQA
        if hh % 2 != kh:
            tile = pltpu.roll(tile, HEAD_DIM, 1)
        return jnp.where(_own_lanes(q_lane, kh), tile, 0.0).astype(BF16)

    q_unit = [jnp.concatenate([placed(hh)
                               for hh in range(u * HEADS_PER_UNIT, (u + 1) * HEADS_PER_UNIT)],
                              axis=0) for u in range(N_UNITS)]
    unit_kv = [u * HEADS_PER_UNIT // GQA for u in range(N_UNITS)]
    gates_t = jax.nn.sigmoid(g_ref[0] + gb_ref[...]).T

    def q_pos(rows):
        return t0 + (lax.broadcasted_iota(jnp.int32, (rows, unit_w), 1) & (Q_BLOCK - 1))

    def value_products(v_ref, chunk0, n_chunks, p_of_unit):
        v_aug = [jnp.concatenate([v_ref[chunk0 + j, kh] for j in range(n_chunks)], axis=1)
                 if n_chunks > 1 else v_ref[chunk0, kh] for kh in range(N_KV)]
        return [_dot(v_aug[unit_kv[u]], p_of_unit(u)) for u in range(N_UNITS)]

    w0 = pl.multiple_of(jnp.maximum(t0 - WINDOW, 0), Q_BLOCK)
    kc = kc_ref[0]
    kw = kw_ref[0, pl.ds(w0, WIN_SPAN), :]
    kd = ks_ref[0, pl.ds(pl.multiple_of(t0, Q_BLOCK), Q_BLOCK), :]
    scores_c = [_dot_nt(kc, q_unit[u]) for u in range(N_UNITS)]
    scores_w = [_dot_nt(kw, q_unit[u]) for u in range(N_UNITS)]
    scores_d = [_dot_nt(kd, q_unit[u]) for u in range(N_UNITS)]

    n_idx = lax.broadcasted_iota(jnp.int32, (n_cmp_pad, unit_w), 0)
    cmp_mask = (n_idx * CMP_STRIDE + (CMP_BLOCK - 1)) <= q_pos(n_cmp_pad)
    sees_a_block = q_pos(1) >= CMP_BLOCK - 1
    p_c = []
    p_sum = [None] * N_KV
    for u in range(N_UNITS):
        kh = unit_kv[u]
        s = jnp.where(cmp_mask, scores_c[u], NEG)
        m = jnp.max(s, axis=0, keepdims=True)
        p = jnp.exp2(s - m)
        l = jnp.sum(p, axis=0, keepdims=True)
        p = p * jnp.where(sees_a_block, 1.0 / l, 0.0)
        p_c.append(p.astype(BF16))
        for hh in range(HEADS_PER_UNIT):
            ph = p[:, hh * Q_BLOCK:(hh + 1) * Q_BLOCK]
            p_sum[kh] = ph if p_sum[kh] is None else p_sum[kh] + ph

    diff = q_pos(WIN_SPAN) - (w0 + lax.broadcasted_iota(jnp.int32, (WIN_SPAN, unit_w), 0))
    win_mask = (diff >= 0) & (diff < WINDOW)
    p_w = [_softmax_cols(scores_w[u], win_mask) for u in range(N_UNITS)]

    key_d = lax.broadcasted_iota(jnp.int32, (Q_BLOCK, unit_w), 0)
    causal_d = (t0 + key_d) <= q_pos(Q_BLOCK)
    m_d, p_d = [], []
    for u in range(N_UNITS):
        s = jnp.where(causal_d, scores_d[u], NEG)
        m_d.append(jnp.max(s, axis=0, keepdims=True))
        p_d.append(jnp.exp2(s - m_d[u]).astype(BF16))

    vc_t = jnp.concatenate([vc_t_ref[i] for i in range(n_cmp_pad // LANES)], axis=1)
    o_c = [_dot(vc_t[unit_kv[u] * HEAD_DIM:(unit_kv[u] + 1) * HEAD_DIM], p_c[u])
           for u in range(N_UNITS)]
    r_w = value_products(vw_t_ref, w0 // LANES, WIN_SPAN // LANES, lambda u: p_w[u])
    acc_d = value_products(vs_t_ref, c, 1, lambda u: p_d[u])

    ratio = SEL_BLOCK // CMP_STRIDE
    span = CMP_BLOCK // CMP_STRIDE
    j_idx = lax.broadcasted_iota(jnp.int32, (SUBLANES, Q_BLOCK), 0)
    cur = (t0 + lax.broadcasted_iota(jnp.int32, (SUBLANES, Q_BLOCK), 1)) // SEL_BLOCK
    first_own_block = t0 // SEL_BLOCK
    n_grp = S_BLOCKS // SUBLANES
    q_aug = [None] * N_UNITS
    for kh in range(N_KV):
        psum_ref[kh, :SUBLANES, :] = jnp.zeros((SUBLANES, Q_BLOCK), F32)
        psum_ref[kh, SUBLANES:, :] = p_sum[kh]
        imp_t = psum_ref[kh, pl.ds(SUBLANES - (span - 1), S_BLOCKS, stride=ratio), :]
        for k in range(2 - span, ratio):
            imp_t = imp_t + psum_ref[kh, pl.ds(SUBLANES + k, S_BLOCKS, stride=ratio), :]
        score = []
        for r in range(n_grp):
            j = j_idx + r * SUBLANES
            forced = (j == 0) | (j == cur) | (j == cur - 1)
            sc = jnp.where(j <= cur, imp_t[r * SUBLANES:(r + 1) * SUBLANES], -FORCE)
            score.append(jnp.where(forced, FORCE, sc))
        bias_rows = []
        for r in range(n_grp):
            cnt = jnp.zeros((SUBLANES, Q_BLOCK), jnp.int32)
            for jp in range(S_BLOCKS):
                rp = jp // SUBLANES
                other = score[rp][jp % SUBLANES:jp % SUBLANES + 1, :]
                ge = (other >= score[r]).astype(jnp.int32)
                gt = (other > score[r]).astype(jnp.int32)
                if rp < r:
                    ahead = ge
                elif rp > r:
                    ahead = gt
                else:
                    ahead = jnp.where(j_idx > (jp % SUBLANES), ge, gt)
                cnt = cnt + ahead
            past = (j_idx + r * SUBLANES) < first_own_block
            bias_rows.append(jnp.where(past, jnp.where(cnt < N_SELECT, 0.0, NEG), NEG))
        zero_rows = [jnp.zeros((LANES - S_BLOCKS, Q_BLOCK), F32)]
        bias_t = jnp.concatenate(
            zero_rows + bias_rows if _spare_lane0(kh) else bias_rows + zero_rows, axis=0)
        bias_q = jnp.concatenate([bias_t.T.astype(BF16)] * HEADS_PER_UNIT, axis=0)
        own = _own_lanes(lax.broadcasted_iota(jnp.int32, bias_q.shape, 1), kh)
        for u in range(N_UNITS):
            if unit_kv[u] == kh:
                q_aug[u] = jnp.where(own, q_unit[u], bias_q)

    o_w = [_normalize(r) for r in r_w]

    chunks = SEL_TILE // LANES
    last_tile = seq // SEL_TILE - 1

    def score_tile(tile, s_ref):
        k0 = pl.multiple_of(tile * SEL_TILE, SEL_TILE)
        kt = [ks_aug_ref[kh, pl.ds(k0, SEL_TILE), :] for kh in range(N_KV)]
        for u in range(N_UNITS):
            s_ref[u] = _dot_nt(kt[unit_kv[u]], q_aug[u])

    def stage(tile, s_cur, s_next, p_cur, p_prev, states, prefetch=True):
        pv_prev = value_products(vs_t_ref, jnp.maximum(tile - 1, 0) * chunks, chunks,
                                 lambda u: p_prev[u])
        if prefetch:
            score_tile(jnp.minimum(tile + 1, last_tile), s_next)
        out = []
        for u in range(N_UNITS):
            m, alpha_prev, acc = states[u]
            s = s_cur[u]
            m_new = jnp.maximum(m, jnp.max(s, axis=0, keepdims=True))
            p_cur[u] = jnp.exp2(s - m_new).astype(BF16)
            out.append((m_new, jnp.exp2(m - m_new), alpha_prev * acc + pv_prev[u]))
        return tuple(out)

    def trip(first_tile, states, prefetch_after=True):
        states = stage(first_tile, s_a_ref, s_b_ref, p_a_ref, p_b_ref, states)
        return stage(first_tile + 1, s_b_ref, s_a_ref, p_b_ref, p_a_ref, states, prefetch_after)

    blocks_per_trip = 2 * SEL_TILE // Q_BLOCK
    loop_trips = jnp.maximum((c + blocks_per_trip - 1) // blocks_per_trip - 2, 0)
    score_tile(0, s_a_ref)
    p_b_ref[...] = jnp.zeros(p_b_ref.shape, BF16)
    init = tuple((m_d[u], jnp.ones((1, unit_w), F32), acc_d[u]) for u in range(N_UNITS))
    sel_states = trip(0, init)
    sel_states = lax.fori_loop(0, loop_trips, lambda j, st: trip(2 * (j + 1), st), sel_states)
    sel_states = trip(2 * (loop_trips + 1), sel_states, prefetch_after=False)
    pv_last = value_products(vs_t_ref, (2 * loop_trips + 3) * chunks, chunks,
                             lambda u: p_b_ref[u])
    o_s = [_normalize(alpha * acc + pv_last[u]) for u, (_, alpha, acc) in enumerate(sel_states)]

    mixed = []
    for hh in range(H_B):
        u, sl = hh // HEADS_PER_UNIT, slice((hh % HEADS_PER_UNIT) * Q_BLOCK,
                                            (hh % HEADS_PER_UNIT + 1) * Q_BLOCK)
        mixed.append(gates_t[hh:hh + 1] * o_c[u][:, sl]
                     + gates_t[H_B + hh:H_B + hh + 1] * o_s[u][:, sl]
                     + gates_t[2 * H_B + hh:2 * H_B + hh + 1] * o_w[u][:, sl])
    bmix = jnp.concatenate(mixed, axis=0).T
    o_ref[0] = (bmix * _rms_scale(bmix) * onw_ref[...]).astype(BF16)


def _nsa(q3, kcv, kv3, g3, gate_b, onw):
    b, s, _ = q3.shape
    n_cmp_pad = kcv.shape[2]
    unit_w = HEADS_PER_UNIT * Q_BLOCK
    const2 = lambda bi, ci: (0, 0)
    kv_spec = lambda col: pl.BlockSpec((1, s, D_KV), lambda bi, ci: (bi, 0, col))
    return pl.pallas_call(
        _nsa_kernel,
        grid=(b, s // Q_BLOCK),
        in_specs=[
            pl.BlockSpec((1, Q_BLOCK, D_B), lambda bi, ci: (bi, ci, 0)),
            pl.BlockSpec((None, 1, n_cmp_pad, D_KV), lambda bi, ci: (bi, 0, 0, 0)),
            pl.BlockSpec((None, 1, n_cmp_pad, D_KV), lambda bi, ci: (bi, 1, 0, 0)),
            kv_spec(0), kv_spec(1), kv_spec(2), kv_spec(3),
            pl.BlockSpec((1, Q_BLOCK, LANES), lambda bi, ci: (bi, ci, 0)),
            pl.BlockSpec((1, LANES), const2),
            pl.BlockSpec((1, D_B), const2),
        ],
        out_specs=pl.BlockSpec((1, Q_BLOCK, D_B), lambda bi, ci: (bi, ci, 0)),
        out_shape=jax.ShapeDtypeStruct((b, s, D_B), BF16),
        scratch_shapes=[
            pltpu.VMEM((N_KV, s, LANES), BF16),
            pltpu.VMEM((s // LANES, N_KV, V_ROWS, LANES), BF16),
            pltpu.VMEM((s // LANES, N_KV, V_ROWS, LANES), BF16),
            pltpu.VMEM((n_cmp_pad // LANES, D_KV, LANES), BF16),
            pltpu.VMEM((N_KV, SUBLANES + n_cmp_pad, Q_BLOCK), F32),
            pltpu.VMEM((N_UNITS, SEL_TILE, unit_w), F32),
            pltpu.VMEM((N_UNITS, SEL_TILE, unit_w), F32),
            pltpu.VMEM((N_UNITS, SEL_TILE, unit_w), BF16),
            pltpu.VMEM((N_UNITS, SEL_TILE, unit_w), BF16),
        ],
        compiler_params=pltpu.CompilerParams(
            dimension_semantics=("arbitrary", "arbitrary"), vmem_limit_bytes=VMEM_LIMIT_BYTES),
        name="nsa",
    )(q3, kcv, kcv, kv3, kv3, kv3, kv3, g3, gate_b, onw)


def _mix_ffn_kernel(x_ref, a_ref, b_ref, nw_ref, fnw_ref, woa_hbm, wob_hbm, wg_hbm, wu_hbm,
                    wd_hbm, o_ref, woa_ref, wob_ref, wg_ref, wu_ref, wd_ref, sem, *, final_norm):
    @pl.when(pl.program_id(0) == 0)
    def _load_weights():
        pairs = ((woa_hbm, woa_ref), (wob_hbm, wob_ref), (wg_hbm, wg_ref), (wu_hbm, wu_ref),
                 (wd_hbm, wd_ref))
        copies = [pltpu.make_async_copy(src, dst, sem.at[i]) for i, (src, dst) in enumerate(pairs)]
        for cp in copies:
            cp.start()
        for cp in copies:
            cp.wait()

    x1 = x_ref[...] + _dot(a_ref[...], woa_ref[...]) + _dot(b_ref[...], wob_ref[...])
    h = (x1 * _rms_scale(x1) * nw_ref[...]).astype(BF16)
    gate = _dot(h, wg_ref[...])
    up = _dot(h, wu_ref[...])
    act = (gate * jax.nn.sigmoid(gate) * up).astype(BF16)
    acc = x1 + _dot(act, wd_ref[...])
    if final_norm:
        acc = acc * _rms_scale(acc) * fnw_ref[...]
    o_ref[...] = acc


def _mix_ffn(x2d, a2d, b2d, woa, wob, nw, wg, wu, wd, fnw, final_norm):
    t = x2d.shape[0]
    row = lambda i: (i, 0)
    const = lambda i: (0, 0)
    weights = (woa, wob, wg, wu, wd)
    in_hbm = pl.BlockSpec(memory_space=pl.ANY)
    return pl.pallas_call(
        functools.partial(_mix_ffn_kernel, final_norm=final_norm),
        grid=(t // TM_FFN,),
        in_specs=[
            pl.BlockSpec((TM_FFN, D_MODEL), row),
            pl.BlockSpec((TM_FFN, D_A), row),
            pl.BlockSpec((TM_FFN, D_B), row),
            pl.BlockSpec((1, D_MODEL), const),
            pl.BlockSpec((1, D_MODEL), const),
        ] + [in_hbm] * len(weights),
        out_specs=pl.BlockSpec((TM_FFN, D_MODEL), row),
        out_shape=jax.ShapeDtypeStruct((t, D_MODEL), F32),
        scratch_shapes=[pltpu.VMEM(wt.shape, wt.dtype) for wt in weights]
        + [pltpu.SemaphoreType.DMA((len(weights),))],
        compiler_params=pltpu.CompilerParams(
            dimension_semantics=("arbitrary",), vmem_limit_bytes=VMEM_LIMIT_BYTES),
        name="mix_ffn",
    )(x2d, a2d, b2d, nw, fnw, *weights)


def kernel(x, norm_mix_w, w_in, gmlp_norm_w, gmlp_ws, gmlp_bs, cmp_pos_k, cmp_pos_v, cmp_k_w1,
           cmp_k_w2, cmp_v_w1, cmp_v_w2, gate_b, out_norm_a_w, out_norm_b_w, w_o, norm_ffn_w,
           w_gate, w_up, w_down, final_norm_w):
    b, s, d = x.shape
    assert d == D_MODEL and s // SEL_BLOCK == S_BLOCKS and s % Q_BLOCK == 0
    assert (b * s) % TM_PROJ == 0 and (b * s) % TM_FFN == 0 and s % (GMLP_CHUNKS * CHUNK) == 0
    t = b * s
    o_uv, o_q = 2 * D_A, 2 * D_A + D_B
    o_c, o_kv = o_q + 2 * D_KV, o_q + 6 * D_KV

    x2d = x.reshape(t, d)
    for l in range(DEPTH):
        w = w_in[l]
        wg_pad = jnp.pad(w[:, o_kv:], ((0, 0), (0, LANES - N_GATES * H_B)))
        bst = jnp.pad(gmlp_bs[l].T, ((0, 0), (0, LANES - H_A)))
        a_n, q, kc_raw, kv, g_raw = _in_proj(
            x2d, norm_mix_w[l][None, :], w[:, :o_uv].astype(BF16), w[:, o_uv:o_q].astype(BF16),
            w[:, o_q:o_c].astype(BF16), w[:, o_c:o_kv].astype(BF16), wg_pad.astype(BF16),
            gmlp_norm_w[l][None, :], gmlp_ws[l], bst, out_norm_a_w[l][None, :])

        kv3 = kv.reshape(b, s, 4 * D_KV)
        pos = jnp.tile(jnp.stack([cmp_pos_k[l], cmp_pos_v[l]]), (1, 1, N_KV))
        w1 = jnp.stack([cmp_k_w1[l], cmp_v_w1[l]]).reshape(2, CMP_BLOCK, HEAD_DIM, CMP_HIDDEN)
        w1 = _per_kv_head_blockdiag(w1).astype(BF16)
        w2 = _per_kv_head_blockdiag(jnp.stack([cmp_k_w2[l], cmp_v_w2[l]])).astype(BF16)
        kcv = _compress(kc_raw.reshape(b, s, 2 * D_KV), pos, w1, w2)

        gb_pad = jnp.pad(gate_b[l], (0, LANES - N_GATES * H_B))[None, :]
        b_n = _nsa(q.reshape(b, s, D_B), kcv, kv3, g_raw.reshape(b, s, LANES), gb_pad,
                   out_norm_b_w[l][None, :])

        x2d = _mix_ffn(
            x2d, a_n.reshape(t, D_A), b_n.reshape(t, D_B),
            w_o[l][:D_A].astype(BF16), w_o[l][D_A:].astype(BF16), norm_ffn_w[l][None, :],
            w_gate[l].astype(BF16), w_up[l].astype(BF16), w_down[l].astype(BF16),
            final_norm_w[None, :], final_norm=(l == DEPTH - 1))
    return x2d.reshape(b, s, d)
```
